```python
import math
import jax, jax.numpy as jnp
from jax import lax
import numpy as np

D_MODEL = 1024
BATCH = 8
SEQ = 2048
DEPTH = 2

HEAD_DIM = 64
N_DIFF_HEADS = 8
DIFF_QK_DIM = 32
N_NA_HEADS = 8
DIFF_QK_W = N_DIFF_HEADS * 2 * DIFF_QK_DIM
DIFF_V_W = N_DIFF_HEADS * HEAD_DIM
NA_W = N_NA_HEADS * HEAD_DIM
MIX_WIDTH = DIFF_V_W + NA_W
IN_PROJ_W = 2 * DIFF_QK_W + DIFF_V_W + 3 * NA_W
GRID_W = 64
NA_WIN_H = 8
NA_WIN_W = 16
T5_BUCKETS = 32
T5_MAX_EXACT = 8
T5_MAX_DIST = 128
Q_BLOCK = 128
PEER_HEADS = 8
PEER_N_KEYS = 128
PEER_N_EXPERTS = PEER_N_KEYS * PEER_N_KEYS
PEER_KEY_DIM = 256
PEER_TOPK = 16
PEER_CHUNK = 128
LN_EPS = 1e-5
NEG_INF = -1e30

kernel_name = "hymba_diff_natten_peer_deepnorm_encoder"


def layer_norm(x, g, b):
    xf = x.astype(jnp.float32)
    mu = jnp.mean(xf, axis=-1, keepdims=True)
    var = jnp.mean(jnp.square(xf - mu), axis=-1, keepdims=True)
    y = (xf - mu) * lax.rsqrt(var + LN_EPS) * g.astype(jnp.float32) + b.astype(jnp.float32)
    return y.astype(x.dtype)


def rms_norm(x, g):
    xf = x.astype(jnp.float32)
    y = xf * lax.rsqrt(jnp.mean(jnp.square(xf), axis=-1, keepdims=True) + LN_EPS) * g.astype(jnp.float32)
    return y.astype(x.dtype)


def t5_bucket(rel):
    half = T5_BUCKETS // 2
    base = jnp.where(rel > 0, half, 0)
    n = jnp.abs(rel)
    nf = jnp.maximum(n, 1).astype(jnp.float32)
    large = T5_MAX_EXACT + (jnp.log(nf / T5_MAX_EXACT) / math.log(T5_MAX_DIST / T5_MAX_EXACT)
                            * (half - T5_MAX_EXACT)).astype(jnp.int32)
    large = jnp.minimum(large, half - 1)
    return base + jnp.where(n < T5_MAX_EXACT, n, large)


def diff_attention(q, k, v, lam, lam_init, norm_g, t5_bias):
    B, S, H, _, dk = q.shape
    dv = v.shape[-1]
    nb = S // Q_BLOCK
    q = q.transpose(3, 0, 2, 1, 4)
    k = k.transpose(3, 0, 2, 1, 4)
    v = v.transpose(0, 2, 1, 3)
    qb = q.reshape(2, B, H, nb, Q_BLOCK, dk).transpose(3, 0, 1, 2, 4, 5)
    kpos = jnp.arange(S, dtype=jnp.int32)
    scale = DIFF_QK_DIM ** -0.5

    def block(args):
        qblk, bi = args
        qpos = bi * Q_BLOCK + jnp.arange(Q_BLOCK, dtype=jnp.int32)
        bias = t5_bias[t5_bucket(kpos[None, :] - qpos[:, None])]
        bias = bias.transpose(2, 0, 1).astype(jnp.float32)
        s = jnp.einsum('mbhqd,mbhkd->mbhqk', qblk, k).astype(jnp.float32) * scale + bias
        a = jax.nn.softmax(s, axis=-1)
        w = a[0] - lam * a[1]
        return jnp.einsum('bhqk,bhkd->bhqd', w.astype(v.dtype), v)

    o = lax.map(block, (qb, jnp.arange(nb, dtype=jnp.int32)))
    o = o.transpose(1, 2, 0, 3, 4).reshape(B, H, S, dv)
    o = rms_norm(o, norm_g) * (1.0 - lam_init)
    return o.transpose(0, 2, 1, 3).reshape(B, S, H * dv)


def neighborhood_attention(q, k, v, rpb):
    B, S, H, d = q.shape
    rows = S // GRID_W
    kh = min(NA_WIN_H, rows)

    def grid(t):
        return t.reshape(B, rows, GRID_W, H, d).transpose(0, 3, 1, 2, 4)

    q, k, v = grid(q), grid(k), grid(v)
    r = jnp.arange(rows, dtype=jnp.int32)
    row_idx = jnp.clip(r - kh // 2, 0, rows - kh)[:, None] + jnp.arange(kh, dtype=jnp.int32)[None, :]
    k_rows = k[:, :, row_idx]
    v_rows = v[:, :, row_idx]
    c = jnp.arange(GRID_W, dtype=jnp.int32)
    col_start = jnp.clip(c - NA_WIN_W // 2, 0, GRID_W - NA_WIN_W)
    in_win = (c[None, :] >= col_start[:, None]) & (c[None, :] < col_start[:, None] + NA_WIN_W)
    dr = row_idx - r[:, None] + (NA_WIN_H - 1)
    dc = jnp.clip(c[None, :] - c[:, None] + (NA_WIN_W - 1), 0, 2 * NA_WIN_W - 2)
    bias = rpb[:, dr[:, None, :, None], dc[None, :, None, :]].astype(jnp.float32)
    bias = jnp.where(in_win[None, None, :, None, :], bias, NEG_INF)
    s = jnp.einsum('bhrqd,bhrikd->bhrqik', q, k_rows).astype(jnp.float32) * (d ** -0.5) + bias
    p = jax.nn.softmax(s.reshape(B, H, rows, GRID_W, kh * GRID_W), axis=-1).reshape(s.shape)
    o = jnp.einsum('bhrqik,bhrikd->bhrqd', p.astype(v.dtype), v_rows)
    return o.transpose(0, 2, 3, 1, 4).reshape(B, S, H * d)


def hybrid_mixer(x, w_in, w_out, lq1, lk1, lq2, lk2, lam_init, norm_g, t5_bias, rpb):
    B, S, _ = x.shape
    proj = x @ w_in
    o1 = DIFF_QK_W
    o2 = o1 + DIFF_QK_W
    o3 = o2 + DIFF_V_W
    o4 = o3 + NA_W
    o5 = o4 + NA_W
    dq = proj[..., :o1].reshape(B, S, N_DIFF_HEADS, 2, DIFF_QK_DIM)
    dk = proj[..., o1:o2].reshape(B, S, N_DIFF_HEADS, 2, DIFF_QK_DIM)
    dv = proj[..., o2:o3].reshape(B, S, N_DIFF_HEADS, HEAD_DIM)
    nq = proj[..., o3:o4].reshape(B, S, N_NA_HEADS, HEAD_DIM)
    nk = proj[..., o4:o5].reshape(B, S, N_NA_HEADS, HEAD_DIM)
    nv = proj[..., o5:].reshape(B, S, N_NA_HEADS, HEAD_DIM)
    lam = (jnp.exp(jnp.sum(lq1.astype(jnp.float32) * lk1.astype(jnp.float32)))
           - jnp.exp(jnp.sum(lq2.astype(jnp.float32) * lk2.astype(jnp.float32))) + lam_init)
    o_diff = diff_attention(dq, dk, dv, lam, lam_init, norm_g, t5_bias)
    o_na = neighborhood_attention(nq, nk, nv, rpb)
    return jnp.concatenate([o_diff, o_na], axis=-1) @ w_out


def peer_ffn(x, wq, keys, u_tab, v_tab):
    B, S, D = x.shape
    T = B * S
    xf = x.reshape(T, D)
    q = (xf @ wq).reshape(T, PEER_HEADS, 2, PEER_KEY_DIM // 2).astype(jnp.float32)
    kf = keys.astype(jnp.float32)
    s1 = jnp.einsum('thd,kd->thk', q[:, :, 0], kf[0])
    s2 = jnp.einsum('thd,kd->thk', q[:, :, 1], kf[1])
    v1, i1 = lax.top_k(s1, PEER_TOPK)
    v2, i2 = lax.top_k(s2, PEER_TOPK)
    cand = (v1[..., :, None] + v2[..., None, :]).reshape(T, PEER_HEADS, PEER_TOPK * PEER_TOPK)
    sc, ci = lax.top_k(cand, PEER_TOPK)
    e1 = jnp.take_along_axis(i1, ci // PEER_TOPK, axis=-1)
    e2 = jnp.take_along_axis(i2, ci % PEER_TOPK, axis=-1)
    idx = e1 * PEER_N_KEYS + e2
    g = jax.nn.softmax(sc, axis=-1)
    nc = T // PEER_CHUNK

    def chunk(args):
        xc, ic, gc = args
        u = u_tab[ic]
        h = jax.nn.gelu(jnp.einsum('chkd,cd->chk', u, xc).astype(jnp.float32), approximate=False)
        w = (gc * h).astype(x.dtype)
        return jnp.einsum('chk,chkd->cd', w, v_tab[ic])

    out = lax.map(chunk, (xf.reshape(nc, PEER_CHUNK, D),
                          idx.reshape(nc, PEER_CHUNK, PEER_HEADS, PEER_TOPK),
                          g.reshape(nc, PEER_CHUNK, PEER_HEADS, PEER_TOPK)))
    return out.reshape(B, S, D)


def setup_inputs(seed: int = 0) -> dict:
    key = jax.random.key(seed)
    ks = jax.random.split(key, 20)
    beta = (8 * DEPTH) ** -0.25
    nrm = jax.random.normal
    x = nrm(ks[0], (BATCH, SEQ, D_MODEL), jnp.float32)
    col_scale = jnp.concatenate([jnp.ones((2 * DIFF_QK_W,), jnp.float32),
                                 jnp.full((DIFF_V_W,), beta, jnp.float32),
                                 jnp.ones((2 * NA_W,), jnp.float32),
                                 jnp.full((NA_W,), beta, jnp.float32)])
    w_in = nrm(ks[1], (DEPTH, D_MODEL, IN_PROJ_W), jnp.float32) * (D_MODEL ** -0.5) * col_scale
    w_out = nrm(ks[2], (DEPTH, MIX_WIDTH, D_MODEL), jnp.float32) * (MIX_WIDTH ** -0.5) * beta
    lam_q1 = nrm(ks[3], (DEPTH, DIFF_QK_DIM), jnp.float32) * 0.1
    lam_k1 = nrm(ks[4], (DEPTH, DIFF_QK_DIM), jnp.float32) * 0.1
    lam_q2 = nrm(ks[5], (DEPTH, DIFF_QK_DIM), jnp.float32) * 0.1
    lam_k2 = nrm(ks[6], (DEPTH, DIFF_QK_DIM), jnp.float32) * 0.1
    diff_norm_g = 1.0 + 0.02 * nrm(ks[7], (DEPTH, HEAD_DIM), jnp.float32)
    t5_bias = 0.1 * nrm(ks[8], (T5_BUCKETS, N_DIFF_HEADS), jnp.float32)
    na_rpb = 0.1 * nrm(ks[9], (DEPTH, N_NA_HEADS, 2 * NA_WIN_H - 1, 2 * NA_WIN_W - 1), jnp.float32)
    ln1_g = 1.0 + 0.02 * nrm(ks[10], (DEPTH, D_MODEL), jnp.float32)
    ln1_b = 0.02 * nrm(ks[11], (DEPTH, D_MODEL), jnp.float32)
    peer_wq = nrm(ks[12], (DEPTH, D_MODEL, PEER_HEADS * PEER_KEY_DIM), jnp.float32) * (D_MODEL ** -0.5)
    peer_keys = nrm(ks[13], (DEPTH, 2, PEER_N_KEYS, PEER_KEY_DIM // 2), jnp.float32) * ((PEER_KEY_DIM // 2) ** -0.5)
    peer_u = nrm(ks[14], (DEPTH, PEER_N_EXPERTS, D_MODEL), jnp.float32) * (D_MODEL ** -0.5)
    peer_v = nrm(ks[15], (DEPTH, PEER_N_EXPERTS, D_MODEL), jnp.float32) * beta
    ln2_g = 1.0 + 0.02 * nrm(ks[16], (DEPTH, D_MODEL), jnp.float32)
    ln2_b = 0.02 * nrm(ks[17], (DEPTH, D_MODEL), jnp.float32)
    return {"x": x, "w_in": w_in, "w_out": w_out, "lam_q1": lam_q1, "lam_k1": lam_k1,
            "lam_q2": lam_q2, "lam_k2": lam_k2, "diff_norm_g": diff_norm_g, "t5_bias": t5_bias,
            "na_rpb": na_rpb, "ln1_g": ln1_g, "ln1_b": ln1_b, "peer_wq": peer_wq,
            "peer_keys": peer_keys, "peer_u": peer_u, "peer_v": peer_v,
            "ln2_g": ln2_g, "ln2_b": ln2_b}


def reference(x, w_in, w_out, lam_q1, lam_k1, lam_q2, lam_k2, diff_norm_g, t5_bias,
              na_rpb, ln1_g, ln1_b, peer_wq, peer_keys, peer_u, peer_v, ln2_g, ln2_b):
    alpha = (2 * DEPTH) ** 0.25
    for l in range(DEPTH):
        lam_init = 0.8 - 0.6 * math.exp(-0.3 * l)
        h = hybrid_mixer(x, w_in[l], w_out[l], lam_q1[l], lam_k1[l], lam_q2[l], lam_k2[l],
                         lam_init, diff_norm_g[l], t5_bias, na_rpb[l])
        x = layer_norm(alpha * x + h, ln1_g[l], ln1_b[l])
        f = peer_ffn(x, peer_wq[l], peer_keys[l], peer_u[l], peer_v[l])
        x = layer_norm(alpha * x + f, ln2_g[l], ln2_b[l])
    return x
```

```python
import functools
import math

import jax
import jax.numpy as jnp
from jax import lax
from jax.experimental import pallas as pl
from jax.experimental.pallas import tpu as pltpu

F32 = jnp.float32
BF16 = jnp.bfloat16

D_MODEL = 1024
BATCH = 8
SEQ = 2048
TOKENS = BATCH * SEQ
DEPTH = 2
HEAD_DIM = 64
N_DIFF_HEADS = 8
DIFF_QK_DIM = 32
N_NA_HEADS = 8
IN_PROJ_W = 3072
GRID_W = 64
GRID_ROWS = SEQ // GRID_W
NA_WIN_H = 8
NA_WIN_W = 16
NA_KEYS = NA_WIN_H * GRID_W
T5_BUCKETS = 32
T5_MAX_EXACT = 8
T5_MAX_DIST = 128
PEER_HEADS = 8
PEER_N_KEYS = 128
PEER_N_EXPERTS = PEER_N_KEYS * PEER_N_KEYS
PEER_TOPK = 16
LN_EPS = 1e-5
NEG_INF = -1e30
ALPHA = (2 * DEPTH) ** 0.25

LANES = 128
VMEM_LIMIT = 56 * 1024 * 1024

DQ_BLK, DK_BLK, DV_BLK, NQ_BLK, NK_BLK, NV_BLK = 0, 4, 8, 12, 16, 20

STAIR = [(i, j) for i in range(PEER_TOPK) for j in range(PEER_TOPK)
         if (i + 1) * (j + 1) <= PEER_TOPK]

NT_DIMS = (((1,), (1,)), ((), ()))


def _params(sem, vmem=VMEM_LIMIT):
    return pltpu.CompilerParams(dimension_semantics=sem, vmem_limit_bytes=vmem)


IN_TM = 512


def _in_proj_kernel(x_ref, w_ref, o_ref):
    o_ref[...] = jnp.dot(x_ref[...].astype(BF16), w_ref[...],
                         preferred_element_type=F32).astype(BF16)


def in_proj(x2d, w):
    return pl.pallas_call(
        _in_proj_kernel,
        grid=(TOKENS // IN_TM,),
        in_specs=[pl.BlockSpec((IN_TM, D_MODEL), lambda i: (i, 0)),
                  pl.BlockSpec((D_MODEL, IN_PROJ_W), lambda i: (0, 0))],
        out_specs=pl.BlockSpec((IN_TM, IN_PROJ_W), lambda i: (i, 0)),
        out_shape=jax.ShapeDtypeStruct((TOKENS, IN_PROJ_W), BF16),
        compiler_params=_params(("parallel",)),
        name="in_proj",
    )(x2d, w)


DIFF_TQ = 256
KEY_BLOCKS = SEQ // LANES


def _diff_kernel(lam_ref, q_ref, k_ref, v_ref, tab_ref, g_ref, o_ref, *, lam_init):
    qi = pl.program_id(2)
    lam = lam_ref[0]
    q = q_ref[0].astype(F32) * (DIFF_QK_DIM ** -0.5)
    k = k_ref[0]
    v = v_ref[0]
    lane = lax.broadcasted_iota(jnp.int32, (1, LANES), 1)
    out = None
    for hh in range(2):
        rows = []
        for sb in range(DIFF_TQ // LANES):
            qblk = qi * (DIFF_TQ // LANES) + sb
            tiles = [tab_ref[hh, jnp.clip(kj - qblk, -2, 2) + 2] for kj in range(KEY_BLOCKS)]
            rows.append(jnp.concatenate(tiles, axis=1))
        bias = jnp.concatenate(rows, axis=0)
        acc = None
        for m in range(2):
            c0 = hh * HEAD_DIM + m * DIFF_QK_DIM
            msk = (lane >= c0) & (lane < c0 + DIFF_QK_DIM)
            qs = jnp.where(msk, q, 0.0).astype(BF16)
            s = lax.dot_general(qs, k, NT_DIMS, preferred_element_type=F32) + bias
            e = jnp.exp(s - jnp.max(s, axis=-1, keepdims=True))
            z = jnp.sum(e, axis=-1, keepdims=True)
            if m == 0:
                acc = e * (1.0 / z)
            else:
                acc = acc - e * (lam / z)
        o = jnp.dot(acc.astype(BF16), v, preferred_element_type=F32)
        hm = (lane >= hh * HEAD_DIM) & (lane < (hh + 1) * HEAD_DIM)
        ms = jnp.sum(jnp.where(hm, o * o, 0.0), axis=-1, keepdims=True) * (1.0 / HEAD_DIM)
        o = jnp.where(hm, o * lax.rsqrt(ms + LN_EPS), 0.0)
        out = o if out is None else out + o
    o_ref[0] = (out * g_ref[...] * (1.0 - lam_init)).astype(BF16)


def diff_attention(proj, lam, t5_tab, norm_g2, lam_init):
    nq = SEQ // DIFF_TQ
    return pl.pallas_call(
        functools.partial(_diff_kernel, lam_init=lam_init),
        grid=(BATCH, N_DIFF_HEADS // 2, nq),
        in_specs=[
            pl.BlockSpec(memory_space=pltpu.SMEM),
            pl.BlockSpec((1, DIFF_TQ, LANES), lambda b, h, i: (b, i, DQ_BLK + h)),
            pl.BlockSpec((1, SEQ, LANES), lambda b, h, i: (b, 0, DK_BLK + h)),
            pl.BlockSpec((1, SEQ, LANES), lambda b, h, i: (b, 0, DV_BLK + h)),
            pl.BlockSpec((2, 5, LANES, LANES), lambda b, h, i: (h, 0, 0, 0)),
            pl.BlockSpec((1, LANES), lambda b, h, i: (0, 0)),
        ],
        out_specs=pl.BlockSpec((1, DIFF_TQ, LANES), lambda b, h, i: (b, i, h)),
        out_shape=jax.ShapeDtypeStruct((BATCH, SEQ, N_DIFF_HEADS * HEAD_DIM), BF16),
        compiler_params=_params(("parallel", "parallel", "arbitrary")),
        name="diff_attention",
    )(lam, proj, proj, proj, t5_tab, norm_g2)


def _na_kernel(q_ref, k_ref, v_ref, tab_ref, o_ref):
    lane = lax.broadcasted_iota(jnp.int32, (1, LANES), 1)

    def row(r, carry):
        rs = jnp.clip(r - NA_WIN_H // 2, 0, GRID_ROWS - NA_WIN_H)
        off = r - rs
        q0 = pl.multiple_of(r * GRID_W, GRID_W)
        k0 = pl.multiple_of(rs * GRID_W, GRID_W)
        qr = q_ref[0, pl.ds(q0, GRID_W), :].astype(F32) * (HEAD_DIM ** -0.5)
        kw = k_ref[0, pl.ds(k0, NA_KEYS), :]
        vw = v_ref[0, pl.ds(k0, NA_KEYS), :]
        out = None
        for hh in range(2):
            hm = (lane >= hh * HEAD_DIM) & (lane < (hh + 1) * HEAD_DIM)
            qs = jnp.where(hm, qr, 0.0).astype(BF16)
            s = lax.dot_general(qs, kw, NT_DIMS, preferred_element_type=F32) + tab_ref[hh, off]
            e = jnp.exp(s - jnp.max(s, axis=-1, keepdims=True))
            p = e * (1.0 / jnp.sum(e, axis=-1, keepdims=True))
            o = jnp.dot(p.astype(BF16), vw, preferred_element_type=F32)
            o = jnp.where(hm, o, 0.0)
            out = o if out is None else out + o
        o_ref[0, pl.ds(q0, GRID_W), :] = out.astype(BF16)
        return carry

    lax.fori_loop(0, GRID_ROWS, row, 0)


def na_attention(proj, na_tab):
    return pl.pallas_call(
        _na_kernel,
        grid=(BATCH, N_NA_HEADS // 2),
        in_specs=[
            pl.BlockSpec((1, SEQ, LANES), lambda b, h: (b, 0, NQ_BLK + h)),
            pl.BlockSpec((1, SEQ, LANES), lambda b, h: (b, 0, NK_BLK + h)),
            pl.BlockSpec((1, SEQ, LANES), lambda b, h: (b, 0, NV_BLK + h)),
            pl.BlockSpec((2, NA_WIN_H, GRID_W, NA_KEYS), lambda b, h: (h, 0, 0, 0)),
        ],
        out_specs=pl.BlockSpec((1, SEQ, LANES), lambda b, h: (b, 0, h)),
        out_shape=jax.ShapeDtypeStruct((BATCH, SEQ, N_NA_HEADS * HEAD_DIM), BF16),
        compiler_params=_params(("parallel", "parallel")),
        name="na_attention",
    )(proj, proj, proj, na_tab)


LN_TM = 512


def _layer_norm(y, g, b):
    mu = jnp.mean(y, axis=-1, keepdims=True)
    yc = y - mu
    var = jnp.mean(yc * yc, axis=-1, keepdims=True)
    return yc * lax.rsqrt(var + LN_EPS) * g + b


def _out_ln_kernel(od_ref, on_ref, x_ref, w_ref, g_ref, b_ref, y_ref, yt_ref):
    half = N_DIFF_HEADS * HEAD_DIM
    h = jnp.dot(od_ref[...], w_ref[0:half, :], preferred_element_type=F32)
    h = h + jnp.dot(on_ref[...], w_ref[half:2 * half, :], preferred_element_type=F32)
    y = _layer_norm(ALPHA * x_ref[...] + h, g_ref[...], b_ref[...])
    y_ref[...] = y
    yt_ref[...] = y.T.astype(BF16)


def out_proj_ln(o_diff, o_na, x2d, w, g, b):
    half = N_DIFF_HEADS * HEAD_DIM
    return pl.pallas_call(
        _out_ln_kernel,
        grid=(TOKENS // LN_TM,),
        in_specs=[
            pl.BlockSpec((LN_TM, half), lambda i: (i, 0)),
            pl.BlockSpec((LN_TM, half), lambda i: (i, 0)),
            pl.BlockSpec((LN_TM, D_MODEL), lambda i: (i, 0)),
            pl.BlockSpec((D_MODEL, D_MODEL), lambda i: (0, 0)),
            pl.BlockSpec((1, D_MODEL), lambda i: (0, 0)),
            pl.BlockSpec((1, D_MODEL), lambda i: (0, 0)),
        ],
        out_specs=[pl.BlockSpec((LN_TM, D_MODEL), lambda i: (i, 0)),
                   pl.BlockSpec((D_MODEL, LN_TM), lambda i: (0, i))],
        out_shape=[jax.ShapeDtypeStruct((TOKENS, D_MODEL), F32),
                   jax.ShapeDtypeStruct((D_MODEL, TOKENS), BF16)],
        compiler_params=_params(("parallel",)),
        name="out_proj_ln",
    )(o_diff, o_na, x2d, w, g, b)


RT_TM = 256
EH_ROWS = PEER_N_KEYS * PEER_HEADS
BIG = 3.0e38


def _top16_desc(s):
    vals = []
    prev = None
    for _ in range(PEER_TOPK):
        w = s if prev is None else jnp.where(s < prev[None], s, -jnp.inf)
        prev = jnp.max(w, axis=0)
        vals.append(prev)
    return vals


def _route_kernel(xt_ref, wqt_ref, keh_ref, c1_ref, p1_ref, s2_ref, p2_ref, scr_ref):
    tm = xt_ref.shape[1]
    qt = jnp.dot(wqt_ref[...], xt_ref[...], preferred_element_type=F32).astype(BF16)
    s1 = jnp.dot(keh_ref[0], qt[:EH_ROWS], preferred_element_type=F32)
    s2 = jnp.dot(keh_ref[1], qt[EH_ROWS:], preferred_element_type=F32)
    s1 = s1.reshape(PEER_N_KEYS, PEER_HEADS, tm)
    s2 = s2.reshape(PEER_N_KEYS, PEER_HEADS, tm)
    v1 = _top16_desc(s1)
    v2 = _top16_desc(s2)
    cands = [v1[i] + v2[j] for (i, j) in STAIR]
    top = cands[0]
    thr = top
    for _ in range(PEER_TOPK - 1):
        nxt = None
        for c in cands:
            w = jnp.where(c < thr, c, -jnp.inf)
            nxt = w if nxt is None else jnp.maximum(nxt, w)
        thr = nxt
    z = jnp.zeros_like(top)
    cut = [jnp.full_like(top, BIG) for _ in range(PEER_TOPK)]
    for (i, j), c in zip(STAIR, cands):
        sel = c >= thr
        z = z + jnp.where(sel, jnp.exp(c - top), 0.0)
        cut[i] = jnp.where(sel, jnp.minimum(cut[i], v2[j]), cut[i])
    inv_z = 1.0 / z
    c1 = jnp.full_like(s1, BIG)
    for i in range(PEER_TOPK):
        c1 = jnp.where(s1 == v1[i][None], cut[i][None], c1)
    p1 = jnp.exp(s1 - v1[0][None]) * inv_z[None]
    p2 = jnp.exp(s2 - v2[0][None])
    for val, ref in ((c1, c1_ref), (p1, p1_ref), (s2, s2_ref), (p2, p2_ref)):
        val = val.reshape(EH_ROWS, tm)
        for c in range(tm // LANES):
            scr_ref[c] = val[:, c * LANES:(c + 1) * LANES]
            for h in range(PEER_HEADS):
                ref[h, :, c * LANES:(c + 1) * LANES] = scr_ref[c, pl.ds(h, PEER_N_KEYS, stride=PEER_HEADS), :]


def peer_route(xt, wqt, keh):
    shp = jax.ShapeDtypeStruct((PEER_HEADS, PEER_N_KEYS, TOKENS), F32)
    ospec = pl.BlockSpec((PEER_HEADS, PEER_N_KEYS, RT_TM), lambda i: (0, 0, i))
    return pl.pallas_call(
        _route_kernel,
        grid=(TOKENS // RT_TM,),
        in_specs=[
            pl.BlockSpec((D_MODEL, RT_TM), lambda i: (0, i)),
            pl.BlockSpec((2 * EH_ROWS, D_MODEL), lambda i: (0, 0)),
            pl.BlockSpec((2, EH_ROWS, EH_ROWS), lambda i: (0, 0, 0)),
        ],
        out_specs=[ospec, ospec, ospec, ospec],
        out_shape=[shp, shp, shp, shp],
        scratch_shapes=[pltpu.VMEM((RT_TM // LANES, EH_ROWS, LANES), F32)],
        compiler_params=_params(("parallel",)),
        name="peer_route",
    )(xt, wqt, keh)


PE_TM = 512
PE_E1 = 8
PE_TE = PE_E1 * PEER_N_KEYS


def _gelu(x):
    return x * (lax.erf(x * (1.0 / math.sqrt(2.0))) + 1.0) * 0.5


def _peer_kernel(xt_ref, u_ref, vt_ref, c1_ref, p1_ref, s2_ref, p2_ref, o_ref, h_scr, w_scr):
    j = pl.program_id(1)

    @pl.when(j == 0)
    def _():
        o_ref[...] = jnp.zeros_like(o_ref)

    h_scr[...] = jnp.dot(u_ref[...], xt_ref[...], preferred_element_type=F32)

    for el in range(PE_E1):
        rs = pl.ds(el * PEER_N_KEYS, PEER_N_KEYS)
        for tc in range(PE_TM // LANES):
            ts = pl.ds(tc * LANES, LANES)
            g = jnp.zeros((PEER_N_KEYS, LANES), F32)
            for h in range(PEER_HEADS):
                c1 = c1_ref[h, pl.ds(el, 1), ts]
                p1 = p1_ref[h, pl.ds(el, 1), ts]
                sel = s2_ref[h, :, ts] >= c1
                g = g + jnp.where(sel, p1 * p2_ref[h, :, ts], 0.0)
            w_scr[rs, ts] = (g * _gelu(h_scr[rs, ts])).astype(BF16)
    o_ref[...] += jnp.dot(vt_ref[...], w_scr[...], preferred_element_type=F32)


def peer_experts(xt, u, vt, c1, p1, s2, p2):
    gate_e1 = pl.BlockSpec((PEER_HEADS, PE_E1, PE_TM), lambda i, j: (0, j, i))
    gate_e2 = pl.BlockSpec((PEER_HEADS, PEER_N_KEYS, PE_TM), lambda i, j: (0, 0, i))
    return pl.pallas_call(
        _peer_kernel,
        grid=(TOKENS // PE_TM, PEER_N_KEYS // PE_E1),
        in_specs=[
            pl.BlockSpec((D_MODEL, PE_TM), lambda i, j: (0, i)),
            pl.BlockSpec((PE_TE, D_MODEL), lambda i, j: (j, 0)),
            pl.BlockSpec((D_MODEL, PE_TE), lambda i, j: (0, j)),
            gate_e1, gate_e1, gate_e2, gate_e2,
        ],
        out_specs=pl.BlockSpec((D_MODEL, PE_TM), lambda i, j: (0, i)),
        out_shape=jax.ShapeDtypeStruct((D_MODEL, TOKENS), F32),
        scratch_shapes=[pltpu.VMEM((PE_TE, PE_TM), F32), pltpu.VMEM((PE_TE, PE_TM), BF16)],
        compiler_params=_params(("parallel", "arbitrary")),
        name="peer_experts",
    )(xt, u, vt, c1, p1, s2, p2)


def _res_ln_kernel(x_ref, ft_ref, g_ref, b_ref, y_ref):
    y_ref[...] = _layer_norm(ALPHA * x_ref[...] + ft_ref[...].T, g_ref[...], b_ref[...])


def res_ln(x2d, ft, g, b):
    return pl.pallas_call(
        _res_ln_kernel,
        grid=(TOKENS // LN_TM,),
        in_specs=[
            pl.BlockSpec((LN_TM, D_MODEL), lambda i: (i, 0)),
            pl.BlockSpec((D_MODEL, LN_TM), lambda i: (0, i)),
            pl.BlockSpec((1, D_MODEL), lambda i: (0, 0)),
            pl.BlockSpec((1, D_MODEL), lambda i: (0, 0)),
        ],
        out_specs=pl.BlockSpec((LN_TM, D_MODEL), lambda i: (i, 0)),
        out_shape=jax.ShapeDtypeStruct((TOKENS, D_MODEL), F32),
        compiler_params=_params(("parallel",)),
        name="res_ln",
    )(x2d, ft, g, b)


def _t5_bucket(rel):
    half = T5_BUCKETS // 2
    base = jnp.where(rel > 0, half, 0)
    n = jnp.abs(rel)
    nf = jnp.maximum(n, 1).astype(F32)
    large = T5_MAX_EXACT + (jnp.log(nf / T5_MAX_EXACT) / math.log(T5_MAX_DIST / T5_MAX_EXACT)
                            * (half - T5_MAX_EXACT)).astype(jnp.int32)
    large = jnp.minimum(large, half - 1)
    return base + jnp.where(n < T5_MAX_EXACT, n, large)


def _t5_tiles(t5_bias):
    r = jnp.arange(LANES, dtype=jnp.int32)
    blk = jnp.arange(-2, 3, dtype=jnp.int32)
    rel = blk[:, None, None] * LANES + r[None, None, :] - r[None, :, None]
    tab = t5_bias[_t5_bucket(rel)]
    return tab.transpose(3, 0, 1, 2).astype(F32)


def _na_tiles(rpb):
    off = jnp.arange(NA_WIN_H, dtype=jnp.int32)
    i = jnp.arange(NA_WIN_H, dtype=jnp.int32)
    dr = i[None, :] - off[:, None] + (NA_WIN_H - 1)
    c = jnp.arange(GRID_W, dtype=jnp.int32)
    col_start = jnp.clip(c - NA_WIN_W // 2, 0, GRID_W - NA_WIN_W)
    in_win = (c[None, :] >= col_start[:, None]) & (c[None, :] < col_start[:, None] + NA_WIN_W)
    dc = jnp.clip(c[None, :] - c[:, None] + (NA_WIN_W - 1), 0, 2 * NA_WIN_W - 2)
    bias = rpb[:, dr[:, None, :, None], dc[None, :, None, :]].astype(F32)
    bias = jnp.where(in_win[None, None, :, None, :], bias, NEG_INF)
    return bias.reshape(N_NA_HEADS, NA_WIN_H, GRID_W, NA_KEYS)


def kernel(x, w_in, w_out, lam_q1, lam_k1, lam_q2, lam_k2, diff_norm_g, t5_bias, na_rpb,
           ln1_g, ln1_b, peer_wq, peer_keys, peer_u, peer_v, ln2_g, ln2_b):
    xf = x.reshape(TOKENS, D_MODEL)
    t5_tab = _t5_tiles(t5_bias)
    eye = jnp.eye(PEER_HEADS, dtype=F32)
    for l in range(DEPTH):
        lam_init = 0.8 - 0.6 * math.exp(-0.3 * l)
        lam = (jnp.exp(jnp.sum(lam_q1[l] * lam_k1[l])) - jnp.exp(jnp.sum(lam_q2[l] * lam_k2[l]))
               + lam_init).reshape(1).astype(F32)
        proj = in_proj(xf, w_in[l].astype(BF16)).reshape(BATCH, SEQ, IN_PROJ_W)
        g2 = jnp.tile(diff_norm_g[l], 2).reshape(1, LANES)
        o_diff = diff_attention(proj, lam, t5_tab, g2, lam_init)
        o_na = na_attention(proj, _na_tiles(na_rpb[l]))
        x1, x1t = out_proj_ln(o_diff.reshape(TOKENS, -1), o_na.reshape(TOKENS, -1), xf,
                              w_out[l].astype(BF16), ln1_g[l].reshape(1, -1), ln1_b[l].reshape(1, -1))
        wqt = (peer_wq[l].reshape(D_MODEL, PEER_HEADS, 2, PEER_N_KEYS)
               .transpose(2, 1, 3, 0).reshape(2 * EH_ROWS, D_MODEL).astype(BF16))
        keh = jnp.einsum('aed,hg->aehgd', peer_keys[l], eye).reshape(2, EH_ROWS, EH_ROWS).astype(BF16)
        c1, p1, s2, p2 = peer_route(x1t, wqt, keh)
        ft = peer_experts(x1t, peer_u[l].astype(BF16), peer_v[l].T.astype(BF16), c1, p1, s2, p2)
        xf = res_ln(x1, ft, ln2_g[l].reshape(1, -1), ln2_b[l].reshape(1, -1))
    return xf.reshape(BATCH, SEQ, D_MODEL)
```

```python
import functools
import math

import jax
import jax.numpy as jnp
from jax import lax
from jax.experimental import pallas as pl
from jax.experimental.pallas import tpu as pltpu

F32 = jnp.float32
BF16 = jnp.bfloat16

D_MODEL = 1024
BATCH = 8
SEQ = 2048
TOKENS = BATCH * SEQ
DEPTH = 2
HEAD_DIM = 64
N_DIFF_HEADS = 8
DIFF_QK_DIM = 32
N_NA_HEADS = 8
IN_PROJ_W = 3072
GRID_W = 64
GRID_ROWS = SEQ // GRID_W
NA_WIN_H = 8
NA_WIN_W = 16
NA_KEYS = NA_WIN_H * GRID_W
T5_BUCKETS = 32
T5_MAX_EXACT = 8
T5_MAX_DIST = 128
PEER_HEADS = 8
PEER_N_KEYS = 128
PEER_N_EXPERTS = PEER_N_KEYS * PEER_N_KEYS
PEER_TOPK = 16
LN_EPS = 1e-5
NEG_INF = -1e30
ALPHA = (2 * DEPTH) ** 0.25

LANES = 128
VMEM_LIMIT = 56 * 1024 * 1024

DQ_BLK, DK_BLK, DV_BLK, NQ_BLK, NK_BLK, NV_BLK = 0, 4, 8, 12, 16, 20

STAIR = [(i, j) for i in range(PEER_TOPK) for j in range(PEER_TOPK)
         if (i + 1) * (j + 1) <= PEER_TOPK]

NT_DIMS = (((1,), (1,)), ((), ()))


def _params(sem, vmem=VMEM_LIMIT):
    return pltpu.CompilerParams(dimension_semantics=sem, vmem_limit_bytes=vmem)


IN_TM = 512


def _in_proj_kernel(x_ref, w_ref, o_ref):
    o_ref[...] = jnp.dot(x_ref[...].astype(BF16), w_ref[...],
                         preferred_element_type=F32).astype(BF16)


def in_proj(x2d, w):
    return pl.pallas_call(
        _in_proj_kernel,
        grid=(TOKENS // IN_TM,),
        in_specs=[pl.BlockSpec((IN_TM, D_MODEL), lambda i: (i, 0)),
                  pl.BlockSpec((D_MODEL, IN_PROJ_W), lambda i: (0, 0))],
        out_specs=pl.BlockSpec((IN_TM, IN_PROJ_W), lambda i: (i, 0)),
        out_shape=jax.ShapeDtypeStruct((TOKENS, IN_PROJ_W), BF16),
        compiler_params=_params(("parallel",)),
        name="in_proj",
    )(x2d, w)


DIFF_TQ = 256
KEY_BLOCKS = SEQ // LANES


def _diff_kernel(lam_ref, q_ref, k_ref, v_ref, tab_ref, g_ref, o_ref, *, lam_init):
    qi = pl.program_id(2)
    lam = lam_ref[0]
    q = q_ref[0].astype(F32) * (DIFF_QK_DIM ** -0.5)
    k = k_ref[0]
    v = v_ref[0]
    lane = lax.broadcasted_iota(jnp.int32, (1, LANES), 1)
    out = None
    for hh in range(2):
        rows = []
        for sb in range(DIFF_TQ // LANES):
            qblk = qi * (DIFF_TQ // LANES) + sb
            tiles = [tab_ref[hh, jnp.clip(kj - qblk, -2, 2) + 2] for kj in range(KEY_BLOCKS)]
            rows.append(jnp.concatenate(tiles, axis=1))
        bias = jnp.concatenate(rows, axis=0)
        acc = None
        for m in range(2):
            c0 = hh * HEAD_DIM + m * DIFF_QK_DIM
            msk = (lane >= c0) & (lane < c0 + DIFF_QK_DIM)
            qs = jnp.where(msk, q, 0.0).astype(BF16)
            s = lax.dot_general(qs, k, NT_DIMS, preferred_element_type=F32) + bias
            e = jnp.exp(s - jnp.max(s, axis=-1, keepdims=True))
            z = jnp.sum(e, axis=-1, keepdims=True)
            if m == 0:
                acc = e * (1.0 / z)
            else:
                acc = acc - e * (lam / z)
        o = jnp.dot(acc.astype(BF16), v, preferred_element_type=F32)
        hm = (lane >= hh * HEAD_DIM) & (lane < (hh + 1) * HEAD_DIM)
        ms = jnp.sum(jnp.where(hm, o * o, 0.0), axis=-1, keepdims=True) * (1.0 / HEAD_DIM)
        o = jnp.where(hm, o * lax.rsqrt(ms + LN_EPS), 0.0)
        out = o if out is None else out + o
    o_ref[0] = (out * g_ref[...] * (1.0 - lam_init)).astype(BF16)


def diff_attention(proj, lam, t5_tab, norm_g2, lam_init):
    nq = SEQ // DIFF_TQ
    return pl.pallas_call(
        functools.partial(_diff_kernel, lam_init=lam_init),
        grid=(BATCH, N_DIFF_HEADS // 2, nq),
        in_specs=[
            pl.BlockSpec(memory_space=pltpu.SMEM),
            pl.BlockSpec((1, DIFF_TQ, LANES), lambda b, h, i: (b, i, DQ_BLK + h)),
            pl.BlockSpec((1, SEQ, LANES), lambda b, h, i: (b, 0, DK_BLK + h)),
            pl.BlockSpec((1, SEQ, LANES), lambda b, h, i: (b, 0, DV_BLK + h)),
            pl.BlockSpec((2, 5, LANES, LANES), lambda b, h, i: (h, 0, 0, 0)),
            pl.BlockSpec((1, LANES), lambda b, h, i: (0, 0)),
        ],
        out_specs=pl.BlockSpec((1, DIFF_TQ, LANES), lambda b, h, i: (b, i, h)),
        out_shape=jax.ShapeDtypeStruct((BATCH, SEQ, N_DIFF_HEADS * HEAD_DIM), BF16),
        compiler_params=_params(("parallel", "parallel", "arbitrary")),
        name="diff_attention",
    )(lam, proj, proj, proj, t5_tab, norm_g2)


def _na_kernel(q_ref, k_ref, v_ref, tab_ref, o_ref):
    lane = lax.broadcasted_iota(jnp.int32, (1, LANES), 1)

    def row(r, carry):
        rs = jnp.clip(r - NA_WIN_H // 2, 0, GRID_ROWS - NA_WIN_H)
        off = r - rs
        q0 = pl.multiple_of(r * GRID_W, GRID_W)
        k0 = pl.multiple_of(rs * GRID_W, GRID_W)
        qr = q_ref[0, pl.ds(q0, GRID_W), :].astype(F32) * (HEAD_DIM ** -0.5)
        kw = k_ref[0, pl.ds(k0, NA_KEYS), :]
        vw = v_ref[0, pl.ds(k0, NA_KEYS), :]
        out = None
        for hh in range(2):
            hm = (lane >= hh * HEAD_DIM) & (lane < (hh + 1) * HEAD_DIM)
            qs = jnp.where(hm, qr, 0.0).astype(BF16)
            s = lax.dot_general(qs, kw, NT_DIMS, preferred_element_type=F32) + tab_ref[hh, off]
            e = jnp.exp(s - jnp.max(s, axis=-1, keepdims=True))
            p = e * (1.0 / jnp.sum(e, axis=-1, keepdims=True))
            o = jnp.dot(p.astype(BF16), vw, preferred_element_type=F32)
            o = jnp.where(hm, o, 0.0)
            out = o if out is None else out + o
        o_ref[0, pl.ds(q0, GRID_W), :] = out.astype(BF16)
        return carry

    lax.fori_loop(0, GRID_ROWS, row, 0)


def na_attention(proj, na_tab):
    return pl.pallas_call(
        _na_kernel,
        grid=(BATCH, N_NA_HEADS // 2),
        in_specs=[
            pl.BlockSpec((1, SEQ, LANES), lambda b, h: (b, 0, NQ_BLK + h)),
            pl.BlockSpec((1, SEQ, LANES), lambda b, h: (b, 0, NK_BLK + h)),
            pl.BlockSpec((1, SEQ, LANES), lambda b, h: (b, 0, NV_BLK + h)),
            pl.BlockSpec((2, NA_WIN_H, GRID_W, NA_KEYS), lambda b, h: (h, 0, 0, 0)),
        ],
        out_specs=pl.BlockSpec((1, SEQ, LANES), lambda b, h: (b, 0, h)),
        out_shape=jax.ShapeDtypeStruct((BATCH, SEQ, N_NA_HEADS * HEAD_DIM), BF16),
        compiler_params=_params(("parallel", "parallel")),
        name="na_attention",
    )(proj, proj, proj, na_tab)


LN_TM = 512


def _layer_norm(y, g, b):
    mu = jnp.mean(y, axis=-1, keepdims=True)
    yc = y - mu
    var = jnp.mean(yc * yc, axis=-1, keepdims=True)
    return yc * lax.rsqrt(var + LN_EPS) * g + b


def _out_ln_kernel(od_ref, on_ref, x_ref, w_ref, g_ref, b_ref, y_ref, yt_ref):
    half = N_DIFF_HEADS * HEAD_DIM
    h = jnp.dot(od_ref[...], w_ref[0:half, :], preferred_element_type=F32)
    h = h + jnp.dot(on_ref[...], w_ref[half:2 * half, :], preferred_element_type=F32)
    y = _layer_norm(ALPHA * x_ref[...] + h, g_ref[...], b_ref[...])
    y_ref[...] = y
    yt_ref[...] = y.T.astype(BF16)


def out_proj_ln(o_diff, o_na, x2d, w, g, b):
    half = N_DIFF_HEADS * HEAD_DIM
    return pl.pallas_call(
        _out_ln_kernel,
        grid=(TOKENS // LN_TM,),
        in_specs=[
            pl.BlockSpec((LN_TM, half), lambda i: (i, 0)),
            pl.BlockSpec((LN_TM, half), lambda i: (i, 0)),
            pl.BlockSpec((LN_TM, D_MODEL), lambda i: (i, 0)),
            pl.BlockSpec((D_MODEL, D_MODEL), lambda i: (0, 0)),
            pl.BlockSpec((1, D_MODEL), lambda i: (0, 0)),
            pl.BlockSpec((1, D_MODEL), lambda i: (0, 0)),
        ],
        out_specs=[pl.BlockSpec((LN_TM, D_MODEL), lambda i: (i, 0)),
                   pl.BlockSpec((D_MODEL, LN_TM), lambda i: (0, i))],
        out_shape=[jax.ShapeDtypeStruct((TOKENS, D_MODEL), F32),
                   jax.ShapeDtypeStruct((D_MODEL, TOKENS), BF16)],
        compiler_params=_params(("parallel",)),
        name="out_proj_ln",
    )(o_diff, o_na, x2d, w, g, b)


RT_TM = 256
EH_ROWS = PEER_N_KEYS * PEER_HEADS
BIG = 3.0e38


def _top16_desc(s):
    vals = []
    prev = None
    for _ in range(PEER_TOPK):
        w = s if prev is None else jnp.where(s < prev[None], s, -jnp.inf)
        prev = jnp.max(w, axis=0)
        vals.append(prev)
    return vals


def _route_kernel(xt_ref, wqt_ref, keh_ref, c1_ref, p1_ref, s2_ref, p2_ref, scr_ref):
    tm = xt_ref.shape[1]
    qt = jnp.dot(wqt_ref[...], xt_ref[...], preferred_element_type=F32).astype(BF16)
    s1 = jnp.dot(keh_ref[0], qt[:EH_ROWS], preferred_element_type=F32)
    s2 = jnp.dot(keh_ref[1], qt[EH_ROWS:], preferred_element_type=F32)
    s1 = s1.reshape(PEER_N_KEYS, PEER_HEADS, tm)
    s2 = s2.reshape(PEER_N_KEYS, PEER_HEADS, tm)
    v1 = _top16_desc(s1)
    v2 = _top16_desc(s2)
    cands = [v1[i] + v2[j] for (i, j) in STAIR]
    top = cands[0]
    thr = top
    for _ in range(PEER_TOPK - 1):
        nxt = None
        for c in cands:
            w = jnp.where(c < thr, c, -jnp.inf)
            nxt = w if nxt is None else jnp.maximum(nxt, w)
        thr = nxt
    z = jnp.zeros_like(top)
    cut = [jnp.full_like(top, BIG) for _ in range(PEER_TOPK)]
    for (i, j), c in zip(STAIR, cands):
        sel = c >= thr
        z = z + jnp.where(sel, jnp.exp(c - top), 0.0)
        cut[i] = jnp.where(sel, jnp.minimum(cut[i], v2[j]), cut[i])
    inv_z = 1.0 / z
    c1 = jnp.full_like(s1, BIG)
    for i in range(PEER_TOPK):
        c1 = jnp.where(s1 == v1[i][None], cut[i][None], c1)
    p1 = jnp.exp(s1 - v1[0][None]) * inv_z[None]
    p2 = jnp.exp(s2 - v2[0][None])
    for val, ref in ((c1, c1_ref), (p1, p1_ref), (s2, s2_ref), (p2, p2_ref)):
        val = val.reshape(EH_ROWS, tm)
        for c in range(tm // LANES):
            scr_ref[c] = val[:, c * LANES:(c + 1) * LANES]
            for h in range(PEER_HEADS):
                ref[h, :, c * LANES:(c + 1) * LANES] = scr_ref[c, pl.ds(h, PEER_N_KEYS, stride=PEER_HEADS), :]


def peer_route(xt, wqt, keh):
    shp = jax.ShapeDtypeStruct((PEER_HEADS, PEER_N_KEYS, TOKENS), F32)
    ospec = pl.BlockSpec((PEER_HEADS, PEER_N_KEYS, RT_TM), lambda i: (0, 0, i))
    return pl.pallas_call(
        _route_kernel,
        grid=(TOKENS // RT_TM,),
        in_specs=[
            pl.BlockSpec((D_MODEL, RT_TM), lambda i: (0, i)),
            pl.BlockSpec((2 * EH_ROWS, D_MODEL), lambda i: (0, 0)),
            pl.BlockSpec((2, EH_ROWS, EH_ROWS), lambda i: (0, 0, 0)),
        ],
        out_specs=[ospec, ospec, ospec, ospec],
        out_shape=[shp, shp, shp, shp],
        scratch_shapes=[pltpu.VMEM((RT_TM // LANES, EH_ROWS, LANES), F32)],
        compiler_params=_params(("parallel",)),
        name="peer_route",
    )(xt, wqt, keh)


PE_TM = 512
PE_E1 = 8
PE_TE = PE_E1 * PEER_N_KEYS


def _gelu(x):
    return x * (lax.erf(x * (1.0 / math.sqrt(2.0))) + 1.0) * 0.5


def _peer_kernel(xt_ref, u_ref, vt_ref, c1_ref, p1_ref, s2_ref, p2_ref, o_ref, h_scr, w_scr):
    j = pl.program_id(1)

    @pl.when(j == 0)
    def _():
        o_ref[...] = jnp.zeros_like(o_ref)

    h_scr[...] = jnp.dot(u_ref[...], xt_ref[...], preferred_element_type=F32)

    for el in range(PE_E1):
        rs = pl.ds(el * PEER_N_KEYS, PEER_N_KEYS)
        for tc in range(PE_TM // LANES):
            ts = pl.ds(tc * LANES, LANES)
            g = jnp.zeros((PEER_N_KEYS, LANES), F32)
            for h in range(PEER_HEADS):
                c1 = c1_ref[h, pl.ds(el, 1), ts]
                p1 = p1_ref[h, pl.ds(el, 1), ts]
                sel = s2_ref[h, :, ts] >= c1
                g = g + jnp.where(sel, p1 * p2_ref[h, :, ts], 0.0)
            w_scr[rs, ts] = (g * _gelu(h_scr[rs, ts])).astype(BF16)
    o_ref[...] += jnp.dot(vt_ref[...], w_scr[...], preferred_element_type=F32)


def peer_experts(xt, u, vt, c1, p1, s2, p2):
    gate_e1 = pl.BlockSpec((PEER_HEADS, PE_E1, PE_TM), lambda i, j: (0, j, i))
    gate_e2 = pl.BlockSpec((PEER_HEADS, PEER_N_KEYS, PE_TM), lambda i, j: (0, 0, i))
    return pl.pallas_call(
        _peer_kernel,
        grid=(TOKENS // PE_TM, PEER_N_KEYS // PE_E1),
        in_specs=[
            pl.BlockSpec((D_MODEL, PE_TM), lambda i, j: (0, i)),
            pl.BlockSpec((PE_TE, D_MODEL), lambda i, j: (j, 0)),
            pl.BlockSpec((D_MODEL, PE_TE), lambda i, j: (0, j)),
            gate_e1, gate_e1, gate_e2, gate_e2,
        ],
        out_specs=pl.BlockSpec((D_MODEL, PE_TM), lambda i, j: (0, i)),
        out_shape=jax.ShapeDtypeStruct((D_MODEL, TOKENS), F32),
        scratch_shapes=[pltpu.VMEM((PE_TE, PE_TM), F32), pltpu.VMEM((PE_TE, PE_TM), BF16)],
        compiler_params=_params(("parallel", "arbitrary")),
        name="peer_experts",
    )(xt, u, vt, c1, p1, s2, p2)


def _res_ln_kernel(x_ref, ft_ref, g_ref, b_ref, y_ref):
    y_ref[...] = _layer_norm(ALPHA * x_ref[...] + ft_ref[...].T, g_ref[...], b_ref[...])


def res_ln(x2d, ft, g, b):
    return pl.pallas_call(
        _res_ln_kernel,
        grid=(TOKENS // LN_TM,),
        in_specs=[
            pl.BlockSpec((LN_TM, D_MODEL), lambda i: (i, 0)),
            pl.BlockSpec((D_MODEL, LN_TM), lambda i: (0, i)),
            pl.BlockSpec((1, D_MODEL), lambda i: (0, 0)),
            pl.BlockSpec((1, D_MODEL), lambda i: (0, 0)),
        ],
        out_specs=pl.BlockSpec((LN_TM, D_MODEL), lambda i: (i, 0)),
        out_shape=jax.ShapeDtypeStruct((TOKENS, D_MODEL), F32),
        compiler_params=_params(("parallel",)),
        name="res_ln",
    )(x2d, ft, g, b)


def _t5_bucket(rel):
    half = T5_BUCKETS // 2
    base = jnp.where(rel > 0, half, 0)
    n = jnp.abs(rel)
    nf = jnp.maximum(n, 1).astype(F32)
    large = T5_MAX_EXACT + (jnp.log(nf / T5_MAX_EXACT) / math.log(T5_MAX_DIST / T5_MAX_EXACT)
                            * (half - T5_MAX_EXACT)).astype(jnp.int32)
    large = jnp.minimum(large, half - 1)
    return base + jnp.where(n < T5_MAX_EXACT, n, large)


def _t5_tiles(t5_bias):
    r = jnp.arange(LANES, dtype=jnp.int32)
    blk = jnp.arange(-2, 3, dtype=jnp.int32)
    rel = blk[:, None, None] * LANES + r[None, None, :] - r[None, :, None]
    onehot = (_t5_bucket(rel)[..., None] == jnp.arange(T5_BUCKETS, dtype=jnp.int32)).astype(F32)
    return jnp.einsum('tqkc,ch->htqk', onehot, t5_bias.astype(F32), precision=lax.Precision.HIGHEST)


def _na_tiles(rpb):
    c = jnp.arange(GRID_W, dtype=jnp.int32)
    col_start = jnp.clip(c - NA_WIN_W // 2, 0, GRID_W - NA_WIN_W)
    in_win = (c[None, :] >= col_start[:, None]) & (c[None, :] < col_start[:, None] + NA_WIN_W)
    dc = jnp.clip(c[None, :] - c[:, None] + (NA_WIN_W - 1), 0, 2 * NA_WIN_W - 2)
    onehot = (dc[..., None] == jnp.arange(2 * NA_WIN_W - 1, dtype=jnp.int32)).astype(F32)
    rows = jnp.stack([rpb[:, NA_WIN_H - 1 - off:2 * NA_WIN_H - 1 - off, :] for off in range(NA_WIN_H)],
                     axis=1).astype(F32)
    bias = jnp.einsum('hoic,qkc->hoqik', rows, onehot, precision=lax.Precision.HIGHEST)
    bias = jnp.where(in_win[None, None, :, None, :], bias, NEG_INF)
    return bias.reshape(N_NA_HEADS, NA_WIN_H, GRID_W, NA_KEYS)


def kernel(x, w_in, w_out, lam_q1, lam_k1, lam_q2, lam_k2, diff_norm_g, t5_bias, na_rpb,
           ln1_g, ln1_b, peer_wq, peer_keys, peer_u, peer_v, ln2_g, ln2_b):
    xf = x.reshape(TOKENS, D_MODEL)
    t5_tab = _t5_tiles(t5_bias)
    eye = jnp.eye(PEER_HEADS, dtype=F32)
    for l in range(DEPTH):
        lam_init = 0.8 - 0.6 * math.exp(-0.3 * l)
        lam = (jnp.exp(jnp.sum(lam_q1[l] * lam_k1[l])) - jnp.exp(jnp.sum(lam_q2[l] * lam_k2[l]))
               + lam_init).reshape(1).astype(F32)
        proj = in_proj(xf, w_in[l].astype(BF16)).reshape(BATCH, SEQ, IN_PROJ_W)
        g2 = jnp.tile(diff_norm_g[l], 2).reshape(1, LANES)
        o_diff = diff_attention(proj, lam, t5_tab, g2, lam_init)
        o_na = na_attention(proj, _na_tiles(na_rpb[l]))
        x1, x1t = out_proj_ln(o_diff.reshape(TOKENS, -1), o_na.reshape(TOKENS, -1), xf,
                              w_out[l].astype(BF16), ln1_g[l].reshape(1, -1), ln1_b[l].reshape(1, -1))
        wqt = (peer_wq[l].reshape(D_MODEL, PEER_HEADS, 2, PEER_N_KEYS)
               .transpose(2, 1, 3, 0).reshape(2 * EH_ROWS, D_MODEL).astype(BF16))
        keh = jnp.einsum('aed,hg->aehgd', peer_keys[l], eye).reshape(2, EH_ROWS, EH_ROWS).astype(BF16)
        c1, p1, s2, p2 = peer_route(x1t, wqt, keh)
        ft = peer_experts(x1t, peer_u[l].astype(BF16), peer_v[l].T.astype(BF16), c1, p1, s2, p2)
        xf = res_ln(x1, ft, ln2_g[l].reshape(1, -1), ln2_b[l].reshape(1, -1))
    return xf.reshape(BATCH, SEQ, D_MODEL)
```

```python
import functools
import math

import jax
import jax.numpy as jnp
from jax import lax
from jax.experimental import pallas as pl
from jax.experimental.pallas import tpu as pltpu

F32 = jnp.float32
BF16 = jnp.bfloat16

D_MODEL = 1024
BATCH = 8
SEQ = 2048
TOKENS = BATCH * SEQ
DEPTH = 2
HEAD_DIM = 64
N_DIFF_HEADS = 8
DIFF_QK_DIM = 32
N_NA_HEADS = 8
IN_PROJ_W = 3072
GRID_W = 64
GRID_ROWS = SEQ // GRID_W
NA_WIN_H = 8
NA_WIN_W = 16
NA_KEYS = NA_WIN_H * GRID_W
T5_BUCKETS = 32
T5_MAX_EXACT = 8
T5_MAX_DIST = 128
PEER_HEADS = 8
PEER_N_KEYS = 128
PEER_N_EXPERTS = PEER_N_KEYS * PEER_N_KEYS
PEER_TOPK = 16
LN_EPS = 1e-5
NEG_INF = -1e30
ALPHA = (2 * DEPTH) ** 0.25
LOG2E = 1.4426950408889634

LANES = 128
VMEM_LIMIT = 56 * 1024 * 1024

DQ_BLK, DK_BLK, DV_BLK, NQ_BLK, NK_BLK, NV_BLK = 0, 4, 8, 12, 16, 20

STAIR = [(i, j) for i in range(PEER_TOPK) for j in range(PEER_TOPK)
         if (i + 1) * (j + 1) <= PEER_TOPK]

NT_DIMS = (((1,), (1,)), ((), ()))
TN_DIMS = (((0,), (0,)), ((), ()))


def _params(sem, vmem=VMEM_LIMIT, flags=None):
    return pltpu.CompilerParams(dimension_semantics=sem, vmem_limit_bytes=vmem, flags=flags)


IN_TM = 512


def _in_proj_kernel(x_ref, w_ref, o_ref):
    o_ref[...] = jnp.dot(x_ref[...].astype(BF16), w_ref[...],
                         preferred_element_type=F32).astype(BF16)


def in_proj(x2d, w):
    return pl.pallas_call(
        _in_proj_kernel,
        grid=(TOKENS // IN_TM,),
        in_specs=[pl.BlockSpec((IN_TM, D_MODEL), lambda i: (i, 0)),
                  pl.BlockSpec((D_MODEL, IN_PROJ_W), lambda i: (0, 0))],
        out_specs=pl.BlockSpec((IN_TM, IN_PROJ_W), lambda i: (i, 0)),
        out_shape=jax.ShapeDtypeStruct((TOKENS, IN_PROJ_W), BF16),
        compiler_params=_params(("parallel",)),
        name="in_proj",
    )(x2d, w)


DIFF_TQ = 256
KEY_BLOCKS = SEQ // LANES


def _diff_kernel(lam_ref, q_ref, k_ref, v_ref, tab_ref, g_ref, o_ref, *, lam_init):
    qi = pl.program_id(2)
    lam = lam_ref[0]
    q = q_ref[0].astype(F32) * (DIFF_QK_DIM ** -0.5 * LOG2E)
    k = k_ref[0]
    v = v_ref[0]
    lane = lax.broadcasted_iota(jnp.int32, (1, LANES), 1)
    out = None
    for hh in range(2):
        rows = []
        for sb in range(DIFF_TQ // LANES):
            qblk = qi * (DIFF_TQ // LANES) + sb
            tiles = [tab_ref[hh, jnp.clip(kj - qblk, -2, 2) + 2] for kj in range(KEY_BLOCKS)]
            rows.append(jnp.concatenate(tiles, axis=1))
        bias = jnp.concatenate(rows, axis=0)
        es, zs = [], []
        for m in range(2):
            c0 = hh * HEAD_DIM + m * DIFF_QK_DIM
            msk = (lane >= c0) & (lane < c0 + DIFF_QK_DIM)
            qs = jnp.where(msk, q, 0.0).astype(BF16)
            s = lax.dot_general(qs, k, NT_DIMS, preferred_element_type=F32) + bias
            e = jnp.exp2(s - jnp.max(s, axis=-1, keepdims=True))
            es.append(e)
            zs.append(jnp.sum(e, axis=-1, keepdims=True))
        w = es[0] - (lam * zs[0] / zs[1]) * es[1]
        o = jnp.dot(w.astype(BF16), v, preferred_element_type=F32) * (1.0 / zs[0])
        hm = (lane >= hh * HEAD_DIM) & (lane < (hh + 1) * HEAD_DIM)
        ms = jnp.sum(jnp.where(hm, o * o, 0.0), axis=-1, keepdims=True) * (1.0 / HEAD_DIM)
        o = jnp.where(hm, o * lax.rsqrt(ms + LN_EPS), 0.0)
        out = o if out is None else out + o
    o_ref[0] = (out * g_ref[...] * (1.0 - lam_init)).astype(BF16)


def diff_attention(proj, lam, t5_tab, norm_g2, lam_init):
    nq = SEQ // DIFF_TQ
    return pl.pallas_call(
        functools.partial(_diff_kernel, lam_init=lam_init),
        grid=(BATCH, N_DIFF_HEADS // 2, nq),
        in_specs=[
            pl.BlockSpec(memory_space=pltpu.SMEM),
            pl.BlockSpec((1, DIFF_TQ, LANES), lambda b, h, i: (b, i, DQ_BLK + h)),
            pl.BlockSpec((1, SEQ, LANES), lambda b, h, i: (b, 0, DK_BLK + h)),
            pl.BlockSpec((1, SEQ, LANES), lambda b, h, i: (b, 0, DV_BLK + h)),
            pl.BlockSpec((2, 5, LANES, LANES), lambda b, h, i: (h, 0, 0, 0)),
            pl.BlockSpec((1, LANES), lambda b, h, i: (0, 0)),
        ],
        out_specs=pl.BlockSpec((1, DIFF_TQ, LANES), lambda b, h, i: (b, i, h)),
        out_shape=jax.ShapeDtypeStruct((BATCH, SEQ, N_DIFF_HEADS * HEAD_DIM), BF16),
        compiler_params=_params(("parallel", "parallel", "arbitrary")),
        name="diff_attention",
    )(lam, proj, proj, proj, t5_tab, norm_g2)


def _na_kernel(q_ref, k_ref, v_ref, tab_ref, o_ref):
    lane = lax.broadcasted_iota(jnp.int32, (1, LANES), 1)
    h0 = lane < HEAD_DIM
    for r in range(GRID_ROWS):
        rs = min(max(r - NA_WIN_H // 2, 0), GRID_ROWS - NA_WIN_H)
        qr = q_ref[0, r * GRID_W:(r + 1) * GRID_W, :].astype(F32) * (HEAD_DIM ** -0.5)
        kw = k_ref[0, rs * GRID_W:rs * GRID_W + NA_KEYS, :]
        vw = v_ref[0, rs * GRID_W:rs * GRID_W + NA_KEYS, :]
        q2 = jnp.concatenate([jnp.where(h0, qr, 0.0), jnp.where(h0, 0.0, qr)], axis=0).astype(BF16)
        s = lax.dot_general(q2, kw, NT_DIMS, preferred_element_type=F32) + tab_ref[0, r - rs]
        e = jnp.exp(s - jnp.max(s, axis=-1, keepdims=True))
        p = e * (1.0 / jnp.sum(e, axis=-1, keepdims=True))
        o = jnp.dot(p.astype(BF16), vw, preferred_element_type=F32)
        o_ref[0, r * GRID_W:(r + 1) * GRID_W, :] = jnp.where(h0, o[:GRID_W], o[GRID_W:]).astype(BF16)


def na_attention(proj, na_tab):
    return pl.pallas_call(
        _na_kernel,
        grid=(BATCH, N_NA_HEADS // 2),
        in_specs=[
            pl.BlockSpec((1, SEQ, LANES), lambda b, h: (b, 0, NQ_BLK + h)),
            pl.BlockSpec((1, SEQ, LANES), lambda b, h: (b, 0, NK_BLK + h)),
            pl.BlockSpec((1, SEQ, LANES), lambda b, h: (b, 0, NV_BLK + h)),
            pl.BlockSpec((1, NA_WIN_H, 2 * GRID_W, NA_KEYS), lambda b, h: (h, 0, 0, 0)),
        ],
        out_specs=pl.BlockSpec((1, SEQ, LANES), lambda b, h: (b, 0, h)),
        out_shape=jax.ShapeDtypeStruct((BATCH, SEQ, N_NA_HEADS * HEAD_DIM), BF16),
        compiler_params=_params(("parallel", "parallel")),
        name="na_attention",
    )(proj, proj, proj, na_tab)


LN_TM = 512


def _layer_norm(y, g, b):
    mu = jnp.mean(y, axis=-1, keepdims=True)
    yc = y - mu
    var = jnp.mean(yc * yc, axis=-1, keepdims=True)
    return yc * lax.rsqrt(var + LN_EPS) * g + b


def _out_ln_kernel(od_ref, on_ref, x_ref, w_ref, g_ref, b_ref, y_ref, yt_ref):
    half = N_DIFF_HEADS * HEAD_DIM
    h = jnp.dot(od_ref[...], w_ref[0:half, :], preferred_element_type=F32)
    h = h + jnp.dot(on_ref[...], w_ref[half:2 * half, :], preferred_element_type=F32)
    y = _layer_norm(ALPHA * x_ref[...] + h, g_ref[...], b_ref[...])
    y_ref[...] = y
    yt_ref[...] = y.T.astype(BF16)


def out_proj_ln(o_diff, o_na, x2d, w, g, b):
    half = N_DIFF_HEADS * HEAD_DIM
    return pl.pallas_call(
        _out_ln_kernel,
        grid=(TOKENS // LN_TM,),
        in_specs=[
            pl.BlockSpec((LN_TM, half), lambda i: (i, 0)),
            pl.BlockSpec((LN_TM, half), lambda i: (i, 0)),
            pl.BlockSpec((LN_TM, D_MODEL), lambda i: (i, 0)),
            pl.BlockSpec((D_MODEL, D_MODEL), lambda i: (0, 0)),
            pl.BlockSpec((1, D_MODEL), lambda i: (0, 0)),
            pl.BlockSpec((1, D_MODEL), lambda i: (0, 0)),
        ],
        out_specs=[pl.BlockSpec((LN_TM, D_MODEL), lambda i: (i, 0)),
                   pl.BlockSpec((D_MODEL, LN_TM), lambda i: (0, i))],
        out_shape=[jax.ShapeDtypeStruct((TOKENS, D_MODEL), F32),
                   jax.ShapeDtypeStruct((D_MODEL, TOKENS), BF16)],
        compiler_params=_params(("parallel",)),
        name="out_proj_ln",
    )(o_diff, o_na, x2d, w, g, b)


RT_TM = 256
EH_ROWS = PEER_N_KEYS * PEER_HEADS


def _cmp_exchange(xs, i, l):
    a, b = xs[i], xs[l]
    xs[i], xs[l] = jnp.maximum(a, b), jnp.minimum(a, b)


def _bitonic_merge_desc(xs):
    xs = list(xs)
    j = len(xs) // 2
    while j >= 1:
        for i in range(len(xs)):
            if i & j == 0:
                _cmp_exchange(xs, i, i | j)
        j //= 2
    return xs


def _sort16_desc(xs):
    xs = list(xs)
    k = 2
    while k <= PEER_TOPK:
        j = k // 2
        while j >= 1:
            for i in range(PEER_TOPK):
                l = i ^ j
                if l > i:
                    if i & k == 0 or k == PEER_TOPK:
                        _cmp_exchange(xs, i, l)
                    else:
                        _cmp_exchange(xs, l, i)
            j //= 2
        k *= 2
    return xs


def _top16_desc(xs):
    runs = [_sort16_desc(xs[g:g + PEER_TOPK]) for g in range(0, len(xs), PEER_TOPK)]
    while len(runs) > 1:
        nxt = []
        for a, b in zip(runs[0::2], runs[1::2]):
            nxt.append(_bitonic_merge_desc([jnp.maximum(a[i], b[PEER_TOPK - 1 - i])
                                            for i in range(PEER_TOPK)]))
        runs = nxt
    return runs[0]


def _route_kernel(xt_ref, wqt_ref, keh_ref, j1_ref, p1_ref, r2_ref, p2_ref, scr_ref):
    tm = xt_ref.shape[1]
    qt = jnp.dot(wqt_ref[...], xt_ref[...], preferred_element_type=F32).astype(BF16)
    s1 = jnp.dot(keh_ref[0], qt[:EH_ROWS], preferred_element_type=F32)
    s2 = jnp.dot(keh_ref[1], qt[EH_ROWS:], preferred_element_type=F32)
    s1 = s1.reshape(PEER_N_KEYS, PEER_HEADS, tm)
    s2 = s2.reshape(PEER_N_KEYS, PEER_HEADS, tm)
    v1 = _top16_desc([s1[e] for e in range(PEER_N_KEYS)])
    v2 = _top16_desc([s2[e] for e in range(PEER_N_KEYS)])
    cands = [v1[i] + v2[j] for (i, j) in STAIR]
    pad = [jnp.full_like(cands[0], -jnp.inf)] * (-len(cands) % PEER_TOPK)
    best = _top16_desc(cands + pad)
    top, thr = best[0], best[PEER_TOPK - 1]
    z = jnp.zeros_like(top)
    cnt = [jnp.zeros_like(top) for _ in range(PEER_TOPK)]
    for (i, j), c in zip(STAIR, cands):
        sel = c >= thr
        z = z + jnp.where(sel, jnp.exp(c - top), 0.0)
        cnt[i] = cnt[i] + jnp.where(sel, 1.0, 0.0)
    inv_z = 1.0 / z
    j1 = jnp.zeros_like(s1)
    r2 = jnp.full_like(s2, float(PEER_TOPK))
    for i in reversed(range(PEER_TOPK)):
        j1 = jnp.where(s1 == v1[i][None], cnt[i][None], j1)
        r2 = jnp.where(s2 >= v2[i][None], float(i), r2)
    p1 = jnp.exp(s1 - v1[0][None]) * inv_z[None]
    p2 = jnp.exp(s2 - v2[0][None])
    j1_ref[...] = j1
    p1_ref[...] = p1
    for val, ref in ((r2, r2_ref), (p2, p2_ref)):
        val = val.reshape(EH_ROWS, tm)
        for c in range(tm // LANES):
            scr_ref[c] = val[:, c * LANES:(c + 1) * LANES]
            for h in range(PEER_HEADS):
                ref[h, :, c * LANES:(c + 1) * LANES] = scr_ref[c, pl.ds(h, PEER_N_KEYS, stride=PEER_HEADS), :]


def peer_route(xt, wqt, keh):
    shp = jax.ShapeDtypeStruct((PEER_HEADS, PEER_N_KEYS, TOKENS), F32)
    ospec = pl.BlockSpec((PEER_HEADS, PEER_N_KEYS, RT_TM), lambda i: (0, 0, i))
    shp_kh = jax.ShapeDtypeStruct((PEER_N_KEYS, PEER_HEADS, TOKENS), F32)
    ospec_kh = pl.BlockSpec((PEER_N_KEYS, PEER_HEADS, RT_TM), lambda i: (0, 0, i))
    return pl.pallas_call(
        _route_kernel,
        grid=(TOKENS // RT_TM,),
        in_specs=[
            pl.BlockSpec((D_MODEL, RT_TM), lambda i: (0, i)),
            pl.BlockSpec((2 * EH_ROWS, D_MODEL), lambda i: (0, 0)),
            pl.BlockSpec((2, EH_ROWS, EH_ROWS), lambda i: (0, 0, 0)),
        ],
        out_specs=[ospec_kh, ospec_kh, ospec, ospec],
        out_shape=[shp_kh, shp_kh, shp, shp],
        scratch_shapes=[pltpu.VMEM((RT_TM // LANES, EH_ROWS, LANES), F32)],
        compiler_params=_params(("parallel",)),
        name="peer_route",
    )(xt, wqt, keh)


PE_TM = 512
PE_E1 = 8
PE_TE = PE_E1 * PEER_N_KEYS


def _gelu(x):
    return x * (lax.erf(x * (1.0 / math.sqrt(2.0))) + 1.0) * 0.5


PE_TILES = PEER_N_KEYS // PE_E1
PE_E1G = 2
PE_CHUNK = PE_E1G * PEER_N_KEYS
PE_NCH = PE_TE // PE_CHUNK
PE_DROWS = D_MODEL // PE_NCH
PE_K2 = 64
PE_STEPS = PE_TILES + 2


def _peer_step(xt_ref, ut_ref, v_ref, j1_ref, p1_ref, r2_ref, p2_ref, o_ref, h_new, h_old, w_new, w_old):
    def chunk(c, carry):
        c0 = pl.multiple_of(c * PE_CHUNK, PE_CHUNK)
        d0 = pl.multiple_of(c * PE_DROWS, PE_DROWS)
        for tc in range(PE_TM // LANES):
            ts = pl.ds(tc * LANES, LANES)
            for k0 in range(0, PEER_N_KEYS, PE_K2):
                g = [None] * PE_E1G
                for h in range(PEER_HEADS):
                    r2 = r2_ref[h, pl.ds(k0, PE_K2), ts]
                    p2 = p2_ref[h, pl.ds(k0, PE_K2), ts]
                    for e in range(PE_E1G):
                        el = c * PE_E1G + e
                        j1 = j1_ref[el, pl.ds(h, 1), ts]
                        p1 = p1_ref[el, pl.ds(h, 1), ts]
                        t = jnp.where(r2 < j1, p1 * p2, 0.0)
                        g[e] = t if g[e] is None else g[e] + t
                for e in range(PE_E1G):
                    rows = pl.ds(c0 + e * PEER_N_KEYS + k0, PE_K2)
                    w_new[rows, ts] = (g[e] * _gelu(h_old[rows, ts])).astype(BF16)
        h_new[pl.ds(c0, PE_CHUNK), :] = lax.dot_general(ut_ref[c], xt_ref[...], TN_DIMS,
                                                        preferred_element_type=F32)
        o_ref[pl.ds(d0, PE_DROWS), :] += lax.dot_general(v_ref[c], w_old[...], TN_DIMS,
                                                         preferred_element_type=F32)
        return carry

    lax.fori_loop(0, PE_NCH, chunk, 0)


def _peer_kernel(xt_ref, ut_ref, v_ref, j1_ref, p1_ref, r2_ref, p2_ref, o_ref, h_a, h_b, w_a, w_b):
    j = pl.program_id(1)

    @pl.when(j == 0)
    def _():
        o_ref[...] = jnp.zeros_like(o_ref)
        h_b[...] = jnp.zeros_like(h_b)
        w_b[...] = jnp.zeros_like(w_b)

    args = (xt_ref, ut_ref, v_ref, j1_ref, p1_ref, r2_ref, p2_ref, o_ref)

    @pl.when(j % 2 == 0)
    def _():
        _peer_step(*args, h_a, h_b, w_a, w_b)

    @pl.when(j % 2 == 1)
    def _():
        _peer_step(*args, h_b, h_a, w_b, w_a)


def peer_experts(xt, ut, v, j1, p1, r2, p2):
    last = PE_TILES - 1
    gate_e1 = pl.BlockSpec((PE_E1, PEER_HEADS, PE_TM), lambda i, j: (jnp.clip(j - 1, 0, last), 0, i))
    gate_e2 = pl.BlockSpec((PEER_HEADS, PEER_N_KEYS, PE_TM), lambda i, j: (0, 0, i))
    return pl.pallas_call(
        _peer_kernel,
        grid=(TOKENS // PE_TM, PE_STEPS),
        in_specs=[
            pl.BlockSpec((D_MODEL, PE_TM), lambda i, j: (0, i)),
            pl.BlockSpec((PE_NCH, D_MODEL, PE_CHUNK), lambda i, j: (jnp.minimum(j, last), 0, 0)),
            pl.BlockSpec((PE_NCH, PE_TE, PE_DROWS), lambda i, j: (jnp.clip(j - 2, 0, last), 0, 0)),
            gate_e1, gate_e1, gate_e2, gate_e2,
        ],
        out_specs=pl.BlockSpec((D_MODEL, PE_TM), lambda i, j: (0, i)),
        out_shape=jax.ShapeDtypeStruct((D_MODEL, TOKENS), F32),
        scratch_shapes=[pltpu.VMEM((PE_TE, PE_TM), F32), pltpu.VMEM((PE_TE, PE_TM), F32),
                        pltpu.VMEM((PE_TE, PE_TM), BF16), pltpu.VMEM((PE_TE, PE_TM), BF16)],
        compiler_params=_params(("parallel", "arbitrary")),
        name="peer_experts",
    )(xt, ut, v, j1, p1, r2, p2)


def _res_ln_kernel(x_ref, ft_ref, g_ref, b_ref, y_ref):
    y_ref[...] = _layer_norm(ALPHA * x_ref[...] + ft_ref[...].T, g_ref[...], b_ref[...])


def res_ln(x2d, ft, g, b):
    return pl.pallas_call(
        _res_ln_kernel,
        grid=(TOKENS // LN_TM,),
        in_specs=[
            pl.BlockSpec((LN_TM, D_MODEL), lambda i: (i, 0)),
            pl.BlockSpec((D_MODEL, LN_TM), lambda i: (0, i)),
            pl.BlockSpec((1, D_MODEL), lambda i: (0, 0)),
            pl.BlockSpec((1, D_MODEL), lambda i: (0, 0)),
        ],
        out_specs=pl.BlockSpec((LN_TM, D_MODEL), lambda i: (i, 0)),
        out_shape=jax.ShapeDtypeStruct((TOKENS, D_MODEL), F32),
        compiler_params=_params(("parallel",)),
        name="res_ln",
    )(x2d, ft, g, b)


def _t5_bucket(rel):
    half = T5_BUCKETS // 2
    base = jnp.where(rel > 0, half, 0)
    n = jnp.abs(rel)
    nf = jnp.maximum(n, 1).astype(F32)
    large = T5_MAX_EXACT + (jnp.log(nf / T5_MAX_EXACT) / math.log(T5_MAX_DIST / T5_MAX_EXACT)
                            * (half - T5_MAX_EXACT)).astype(jnp.int32)
    large = jnp.minimum(large, half - 1)
    return base + jnp.where(n < T5_MAX_EXACT, n, large)


def _t5_tiles(t5_bias):
    r = jnp.arange(LANES, dtype=jnp.int32)
    blk = jnp.arange(-2, 3, dtype=jnp.int32)
    rel = blk[:, None, None] * LANES + r[None, None, :] - r[None, :, None]
    onehot = (_t5_bucket(rel)[..., None] == jnp.arange(T5_BUCKETS, dtype=jnp.int32)).astype(F32)
    return jnp.einsum('tqkc,ch->htqk', onehot, t5_bias.astype(F32), precision=lax.Precision.HIGHEST)


def _na_tiles(rpb):
    c = jnp.arange(GRID_W, dtype=jnp.int32)
    col_start = jnp.clip(c - NA_WIN_W // 2, 0, GRID_W - NA_WIN_W)
    in_win = (c[None, :] >= col_start[:, None]) & (c[None, :] < col_start[:, None] + NA_WIN_W)
    dc = jnp.clip(c[None, :] - c[:, None] + (NA_WIN_W - 1), 0, 2 * NA_WIN_W - 2)
    onehot = (dc[..., None] == jnp.arange(2 * NA_WIN_W - 1, dtype=jnp.int32)).astype(F32)
    rows = jnp.stack([rpb[:, NA_WIN_H - 1 - off:2 * NA_WIN_H - 1 - off, :] for off in range(NA_WIN_H)],
                     axis=1).astype(F32)
    bias = jnp.einsum('hoic,qkc->hoqik', rows, onehot, precision=lax.Precision.HIGHEST)
    bias = jnp.where(in_win[None, None, :, None, :], bias, NEG_INF)
    bias = bias.reshape(N_NA_HEADS // 2, 2, NA_WIN_H, GRID_W, NA_KEYS)
    return bias.transpose(0, 2, 1, 3, 4).reshape(N_NA_HEADS // 2, NA_WIN_H, 2 * GRID_W, NA_KEYS)


def kernel(x, w_in, w_out, lam_q1, lam_k1, lam_q2, lam_k2, diff_norm_g, t5_bias, na_rpb,
           ln1_g, ln1_b, peer_wq, peer_keys, peer_u, peer_v, ln2_g, ln2_b):
    xf = x.reshape(TOKENS, D_MODEL)
    t5_tab = _t5_tiles(t5_bias) * LOG2E
    eye = jnp.eye(PEER_HEADS, dtype=F32)
    for l in range(DEPTH):
        lam_init = 0.8 - 0.6 * math.exp(-0.3 * l)
        lam = (jnp.exp(jnp.sum(lam_q1[l] * lam_k1[l])) - jnp.exp(jnp.sum(lam_q2[l] * lam_k2[l]))
               + lam_init).reshape(1).astype(F32)
        proj = in_proj(xf, w_in[l].astype(BF16)).reshape(BATCH, SEQ, IN_PROJ_W)
        g2 = jnp.tile(diff_norm_g[l], 2).reshape(1, LANES)
        o_diff = diff_attention(proj, lam, t5_tab, g2, lam_init)
        o_na = na_attention(proj, _na_tiles(na_rpb[l]))
        x1, x1t = out_proj_ln(o_diff.reshape(TOKENS, -1), o_na.reshape(TOKENS, -1), xf,
                              w_out[l].astype(BF16), ln1_g[l].reshape(1, -1), ln1_b[l].reshape(1, -1))
        wqt = (peer_wq[l].reshape(D_MODEL, PEER_HEADS, 2, PEER_N_KEYS)
               .transpose(2, 1, 3, 0).reshape(2 * EH_ROWS, D_MODEL).astype(BF16))
        keh = jnp.einsum('aed,hg->aehgd', peer_keys[l], eye).reshape(2, EH_ROWS, EH_ROWS).astype(BF16)
        j1, p1, r2, p2 = peer_route(x1t, wqt, keh)
        ut = (peer_u[l].astype(BF16).reshape(PEER_N_EXPERTS // PE_CHUNK, PE_CHUNK, D_MODEL)
              .transpose(0, 2, 1))
        v = (peer_v[l].astype(BF16).reshape(PE_TILES, PE_TE, PE_NCH, PE_DROWS)
             .transpose(0, 2, 1, 3).reshape(PE_TILES * PE_NCH, PE_TE, PE_DROWS))
        ft = peer_experts(x1t, ut, v, j1, p1, r2, p2)
        xf = res_ln(x1, ft, ln2_g[l].reshape(1, -1), ln2_b[l].reshape(1, -1))
    return xf.reshape(BATCH, SEQ, D_MODEL)
```

```python
import functools
import math

import jax
import jax.numpy as jnp
from jax import lax
from jax.experimental import pallas as pl
from jax.experimental.pallas import tpu as pltpu

F32 = jnp.float32
BF16 = jnp.bfloat16

D_MODEL = 1024
BATCH = 8
SEQ = 2048
TOKENS = BATCH * SEQ
DEPTH = 2
HEAD_DIM = 64
N_DIFF_HEADS = 8
DIFF_QK_DIM = 32
N_NA_HEADS = 8
IN_PROJ_W = 3072
GRID_W = 64
GRID_ROWS = SEQ // GRID_W
NA_WIN_H = 8
NA_WIN_W = 16
NA_KEYS = NA_WIN_H * GRID_W
T5_BUCKETS = 32
T5_MAX_EXACT = 8
T5_MAX_DIST = 128
PEER_HEADS = 8
PEER_N_KEYS = 128
PEER_N_EXPERTS = PEER_N_KEYS * PEER_N_KEYS
PEER_TOPK = 16
LN_EPS = 1e-5
NEG_INF = -1e30
ALPHA = (2 * DEPTH) ** 0.25
LOG2E = 1.4426950408889634

LANES = 128
VMEM_LIMIT = 56 * 1024 * 1024

DQ_BLK, DK_BLK, DV_BLK, NQ_BLK, NK_BLK, NV_BLK = 0, 4, 8, 12, 16, 20

STAIR = [(i, j) for i in range(PEER_TOPK) for j in range(PEER_TOPK)
         if (i + 1) * (j + 1) <= PEER_TOPK]

NT_DIMS = (((1,), (1,)), ((), ()))
TN_DIMS = (((0,), (0,)), ((), ()))


def _params(sem, vmem=VMEM_LIMIT, flags=None):
    return pltpu.CompilerParams(dimension_semantics=sem, vmem_limit_bytes=vmem, flags=flags)


IN_TM = 512


def _in_proj_kernel(x_ref, w_ref, o_ref):
    o_ref[...] = jnp.dot(x_ref[...].astype(BF16), w_ref[...],
                         preferred_element_type=F32).astype(BF16)


def in_proj(x2d, w):
    return pl.pallas_call(
        _in_proj_kernel,
        grid=(TOKENS // IN_TM,),
        in_specs=[pl.BlockSpec((IN_TM, D_MODEL), lambda i: (i, 0)),
                  pl.BlockSpec((D_MODEL, IN_PROJ_W), lambda i: (0, 0))],
        out_specs=pl.BlockSpec((IN_TM, IN_PROJ_W), lambda i: (i, 0)),
        out_shape=jax.ShapeDtypeStruct((TOKENS, IN_PROJ_W), BF16),
        compiler_params=_params(("parallel",)),
        name="in_proj",
    )(x2d, w)


DIFF_TQ = 256
KEY_BLOCKS = SEQ // LANES


def _diff_kernel(lam_ref, q_ref, k_ref, v_ref, tab_ref, g_ref, o_ref, *, lam_init):
    qi = pl.program_id(2)
    lam = lam_ref[0]
    q = q_ref[0].astype(F32) * (DIFF_QK_DIM ** -0.5 * LOG2E)
    k = k_ref[0]
    v = v_ref[0]
    lane = lax.broadcasted_iota(jnp.int32, (1, LANES), 1)
    out = None
    for hh in range(2):
        rows = []
        for sb in range(DIFF_TQ // LANES):
            qblk = qi * (DIFF_TQ // LANES) + sb
            tiles = [tab_ref[hh, jnp.clip(kj - qblk, -2, 2) + 2] for kj in range(KEY_BLOCKS)]
            rows.append(jnp.concatenate(tiles, axis=1))
        bias = jnp.concatenate(rows, axis=0)
        es, zs = [], []
        for m in range(2):
            c0 = hh * HEAD_DIM + m * DIFF_QK_DIM
            msk = (lane >= c0) & (lane < c0 + DIFF_QK_DIM)
            qs = jnp.where(msk, q, 0.0).astype(BF16)
            s = lax.dot_general(qs, k, NT_DIMS, preferred_element_type=F32) + bias
            e = jnp.exp2(s - jnp.max(s, axis=-1, keepdims=True))
            es.append(e)
            zs.append(jnp.sum(e, axis=-1, keepdims=True))
        w = es[0] - (lam * zs[0] / zs[1]) * es[1]
        o = jnp.dot(w.astype(BF16), v, preferred_element_type=F32) * (1.0 / zs[0])
        hm = (lane >= hh * HEAD_DIM) & (lane < (hh + 1) * HEAD_DIM)
        ms = jnp.sum(jnp.where(hm, o * o, 0.0), axis=-1, keepdims=True) * (1.0 / HEAD_DIM)
        o = jnp.where(hm, o * lax.rsqrt(ms + LN_EPS), 0.0)
        out = o if out is None else out + o
    o_ref[0] = (out * g_ref[...] * (1.0 - lam_init)).astype(BF16)


def diff_attention(proj, lam, t5_tab, norm_g2, lam_init):
    nq = SEQ // DIFF_TQ
    return pl.pallas_call(
        functools.partial(_diff_kernel, lam_init=lam_init),
        grid=(BATCH, N_DIFF_HEADS // 2, nq),
        in_specs=[
            pl.BlockSpec(memory_space=pltpu.SMEM),
            pl.BlockSpec((1, DIFF_TQ, LANES), lambda b, h, i: (b, i, DQ_BLK + h)),
            pl.BlockSpec((1, SEQ, LANES), lambda b, h, i: (b, 0, DK_BLK + h)),
            pl.BlockSpec((1, SEQ, LANES), lambda b, h, i: (b, 0, DV_BLK + h)),
            pl.BlockSpec((2, 5, LANES, LANES), lambda b, h, i: (h, 0, 0, 0)),
            pl.BlockSpec((1, LANES), lambda b, h, i: (0, 0)),
        ],
        out_specs=pl.BlockSpec((1, DIFF_TQ, LANES), lambda b, h, i: (b, i, h)),
        out_shape=jax.ShapeDtypeStruct((BATCH, SEQ, N_DIFF_HEADS * HEAD_DIM), BF16),
        compiler_params=_params(("parallel", "parallel", "arbitrary")),
        name="diff_attention",
    )(lam, proj, proj, proj, t5_tab, norm_g2)


def _na_kernel(q_ref, k_ref, v_ref, tab_ref, o_ref):
    lane = lax.broadcasted_iota(jnp.int32, (1, LANES), 1)
    h0 = lane < HEAD_DIM
    for r in range(GRID_ROWS):
        rs = min(max(r - NA_WIN_H // 2, 0), GRID_ROWS - NA_WIN_H)
        qr = q_ref[0, r * GRID_W:(r + 1) * GRID_W, :].astype(F32) * (HEAD_DIM ** -0.5)
        kw = k_ref[0, rs * GRID_W:rs * GRID_W + NA_KEYS, :]
        vw = v_ref[0, rs * GRID_W:rs * GRID_W + NA_KEYS, :]
        q2 = jnp.concatenate([jnp.where(h0, qr, 0.0), jnp.where(h0, 0.0, qr)], axis=0).astype(BF16)
        s = lax.dot_general(q2, kw, NT_DIMS, preferred_element_type=F32) + tab_ref[0, r - rs]
        e = jnp.exp(s - jnp.max(s, axis=-1, keepdims=True))
        p = e * (1.0 / jnp.sum(e, axis=-1, keepdims=True))
        o = jnp.dot(p.astype(BF16), vw, preferred_element_type=F32)
        o_ref[0, r * GRID_W:(r + 1) * GRID_W, :] = jnp.where(h0, o[:GRID_W], o[GRID_W:]).astype(BF16)


def na_attention(proj, na_tab):
    return pl.pallas_call(
        _na_kernel,
        grid=(BATCH, N_NA_HEADS // 2),
        in_specs=[
            pl.BlockSpec((1, SEQ, LANES), lambda b, h: (b, 0, NQ_BLK + h)),
            pl.BlockSpec((1, SEQ, LANES), lambda b, h: (b, 0, NK_BLK + h)),
            pl.BlockSpec((1, SEQ, LANES), lambda b, h: (b, 0, NV_BLK + h)),
            pl.BlockSpec((1, NA_WIN_H, 2 * GRID_W, NA_KEYS), lambda b, h: (h, 0, 0, 0)),
        ],
        out_specs=pl.BlockSpec((1, SEQ, LANES), lambda b, h: (b, 0, h)),
        out_shape=jax.ShapeDtypeStruct((BATCH, SEQ, N_NA_HEADS * HEAD_DIM), BF16),
        compiler_params=_params(("parallel", "parallel")),
        name="na_attention",
    )(proj, proj, proj, na_tab)


LN_TM = 512


def _layer_norm(y, g, b):
    mu = jnp.mean(y, axis=-1, keepdims=True)
    yc = y - mu
    var = jnp.mean(yc * yc, axis=-1, keepdims=True)
    return yc * lax.rsqrt(var + LN_EPS) * g + b


def _out_ln_kernel(od_ref, on_ref, x_ref, w_ref, g_ref, b_ref, y_ref, yt_ref):
    half = N_DIFF_HEADS * HEAD_DIM
    h = jnp.dot(od_ref[...], w_ref[0:half, :], preferred_element_type=F32)
    h = h + jnp.dot(on_ref[...], w_ref[half:2 * half, :], preferred_element_type=F32)
    y = _layer_norm(ALPHA * x_ref[...] + h, g_ref[...], b_ref[...])
    y_ref[...] = y
    yt_ref[...] = y.T.astype(BF16)


def out_proj_ln(o_diff, o_na, x2d, w, g, b):
    half = N_DIFF_HEADS * HEAD_DIM
    return pl.pallas_call(
        _out_ln_kernel,
        grid=(TOKENS // LN_TM,),
        in_specs=[
            pl.BlockSpec((LN_TM, half), lambda i: (i, 0)),
            pl.BlockSpec((LN_TM, half), lambda i: (i, 0)),
            pl.BlockSpec((LN_TM, D_MODEL), lambda i: (i, 0)),
            pl.BlockSpec((D_MODEL, D_MODEL), lambda i: (0, 0)),
            pl.BlockSpec((1, D_MODEL), lambda i: (0, 0)),
            pl.BlockSpec((1, D_MODEL), lambda i: (0, 0)),
        ],
        out_specs=[pl.BlockSpec((LN_TM, D_MODEL), lambda i: (i, 0)),
                   pl.BlockSpec((D_MODEL, LN_TM), lambda i: (0, i))],
        out_shape=[jax.ShapeDtypeStruct((TOKENS, D_MODEL), F32),
                   jax.ShapeDtypeStruct((D_MODEL, TOKENS), BF16)],
        compiler_params=_params(("parallel",)),
        name="out_proj_ln",
    )(o_diff, o_na, x2d, w, g, b)


RT_TM = 512
EH_ROWS = PEER_N_KEYS * PEER_HEADS


def _cmp_exchange(xs, i, l):
    a, b = xs[i], xs[l]
    xs[i], xs[l] = jnp.maximum(a, b), jnp.minimum(a, b)


def _bitonic_merge_desc(xs):
    xs = list(xs)
    j = len(xs) // 2
    while j >= 1:
        for i in range(len(xs)):
            if i & j == 0:
                _cmp_exchange(xs, i, i | j)
        j //= 2
    return xs


def _sort16_desc(xs):
    xs = list(xs)
    k = 2
    while k <= PEER_TOPK:
        j = k // 2
        while j >= 1:
            for i in range(PEER_TOPK):
                l = i ^ j
                if l > i:
                    if i & k == 0 or k == PEER_TOPK:
                        _cmp_exchange(xs, i, l)
                    else:
                        _cmp_exchange(xs, l, i)
            j //= 2
        k *= 2
    return xs


def _top16_desc(xs):
    runs = [_sort16_desc(xs[g:g + PEER_TOPK]) for g in range(0, len(xs), PEER_TOPK)]
    while len(runs) > 1:
        nxt = []
        for a, b in zip(runs[0::2], runs[1::2]):
            nxt.append(_bitonic_merge_desc([jnp.maximum(a[i], b[PEER_TOPK - 1 - i])
                                            for i in range(PEER_TOPK)]))
        runs = nxt
    return runs[0]


BIG = 3.0e38


def _route_kernel(xt_ref, wqt_ref, keh_ref, c1_ref, p1_ref, s2_ref, p2_ref, s_scr, t_scr):
    qt = jnp.dot(wqt_ref[...], xt_ref[...], preferred_element_type=F32).astype(BF16)
    for a in range(2):
        s = jnp.dot(keh_ref[a], qt[a * EH_ROWS:(a + 1) * EH_ROWS], preferred_element_type=F32)
        for b in range(RT_TM // LANES):
            s_scr[a, b] = s[:, b * LANES:(b + 1) * LANES]

    def block(b, carry):
        s1 = s_scr[0, b].reshape(PEER_N_KEYS, PEER_HEADS, LANES)
        s2 = s_scr[1, b].reshape(PEER_N_KEYS, PEER_HEADS, LANES)
        v1 = _top16_desc([s1[e] for e in range(PEER_N_KEYS)])
        v2 = _top16_desc([s2[e] for e in range(PEER_N_KEYS)])
        cands = [v1[i] + v2[j] for (i, j) in STAIR]
        pad = [jnp.full_like(cands[0], -jnp.inf)] * (-len(cands) % PEER_TOPK)
        best = _top16_desc(cands + pad)
        top, thr = best[0], best[PEER_TOPK - 1]
        z = jnp.zeros_like(top)
        cut = [jnp.full_like(top, BIG) for _ in range(PEER_TOPK)]
        for (i, j), c in zip(STAIR, cands):
            sel = c >= thr
            z = z + jnp.where(sel, jnp.exp(c - top), 0.0)
            cut[i] = jnp.where(sel, jnp.minimum(cut[i], v2[j]), cut[i])
        inv_z = 1.0 / z
        c1 = jnp.full_like(s1, BIG)
        for i in range(PEER_TOPK):
            c1 = jnp.where(s1 == v1[i][None], cut[i][None], c1)
        c1_ref[b] = c1
        p1_ref[b] = jnp.exp(s1 - v1[0][None]) * inv_z[None]
        t_scr[...] = jnp.exp(s2 - v2[0][None]).reshape(EH_ROWS, LANES)
        for h in range(PEER_HEADS):
            s2_ref[b, h] = s_scr[1, b, pl.ds(h, PEER_N_KEYS, stride=PEER_HEADS), :]
            p2_ref[b, h] = t_scr[pl.ds(h, PEER_N_KEYS, stride=PEER_HEADS), :]
        return carry

    lax.fori_loop(0, RT_TM // LANES, block, 0)


def peer_route(xt, wqt, keh):
    nblk = TOKENS // LANES
    nb = RT_TM // LANES
    shp_kh = jax.ShapeDtypeStruct((nblk, PEER_N_KEYS, PEER_HEADS, LANES), F32)
    shp_hk = jax.ShapeDtypeStruct((nblk, PEER_HEADS, PEER_N_KEYS, LANES), F32)
    ospec_kh = pl.BlockSpec((nb, PEER_N_KEYS, PEER_HEADS, LANES), lambda i: (i, 0, 0, 0))
    ospec_hk = pl.BlockSpec((nb, PEER_HEADS, PEER_N_KEYS, LANES), lambda i: (i, 0, 0, 0))
    return pl.pallas_call(
        _route_kernel,
        grid=(TOKENS // RT_TM,),
        in_specs=[
            pl.BlockSpec((D_MODEL, RT_TM), lambda i: (0, i)),
            pl.BlockSpec((2 * EH_ROWS, D_MODEL), lambda i: (0, 0)),
            pl.BlockSpec((2, EH_ROWS, EH_ROWS), lambda i: (0, 0, 0)),
        ],
        out_specs=[ospec_kh, ospec_kh, ospec_hk, ospec_hk],
        out_shape=[shp_kh, shp_kh, shp_hk, shp_hk],
        scratch_shapes=[pltpu.VMEM((2, nb, EH_ROWS, LANES), F32), pltpu.VMEM((EH_ROWS, LANES), F32)],
        compiler_params=_params(("parallel",)),
        name="peer_route",
    )(xt, wqt, keh)


PE_TM = 512
PE_E1 = 8
PE_TE = PE_E1 * PEER_N_KEYS


def _gelu(x):
    return x * (lax.erf(x * (1.0 / math.sqrt(2.0))) + 1.0) * 0.5


PE_TILES = PEER_N_KEYS // PE_E1
PE_E1G = 2
PE_CHUNK = PE_E1G * PEER_N_KEYS
PE_NCH = PE_TE // PE_CHUNK
PE_DROWS = D_MODEL // PE_NCH
PE_K2 = 64
PE_STEPS = PE_TILES + 2


def _peer_chunk(c, xt_ref, ut_ref, v_ref, c1_ref, p1_ref, s2_ref, p2_ref, o_ref, h_new, h_old, w_new, w_old):
    c0 = pl.multiple_of(c * PE_CHUNK, PE_CHUNK)
    d0 = pl.multiple_of(c * PE_DROWS, PE_DROWS)
    for tc in range(PE_TM // LANES):
        for k0 in range(0, PEER_N_KEYS, PE_K2):
            g = [None] * PE_E1G
            for h in range(PEER_HEADS):
                s2 = s2_ref[tc, h, pl.ds(k0, PE_K2), :]
                p2 = p2_ref[tc, h, pl.ds(k0, PE_K2), :]
                for e in range(PE_E1G):
                    el = c * PE_E1G + e
                    c1 = c1_ref[tc, el, pl.ds(h, 1), :]
                    p1 = p1_ref[tc, el, pl.ds(h, 1), :]
                    t = jnp.where(s2 >= c1, p1 * p2, 0.0)
                    g[e] = t if g[e] is None else g[e] + t
            for e in range(PE_E1G):
                rows = pl.ds(c0 + e * PEER_N_KEYS + k0, PE_K2)
                ts = pl.ds(tc * LANES, LANES)
                w_new[rows, ts] = (g[e] * _gelu(h_old[rows, ts])).astype(BF16)
    h_new[pl.ds(c0, PE_CHUNK), :] = lax.dot_general(ut_ref[c], xt_ref[...], TN_DIMS,
                                                    preferred_element_type=F32)
    o_ref[pl.ds(d0, PE_DROWS), :] += lax.dot_general(v_ref[c], w_old[...], TN_DIMS,
                                                     preferred_element_type=F32)


def _peer_step(*refs):
    def body(c, carry):
        _peer_chunk(c, *refs)
        return carry

    lax.fori_loop(0, PE_NCH, body, 0)


def _peer_kernel(xt_ref, ut_ref, v_ref, c1_ref, p1_ref, s2_ref, p2_ref, o_ref, h_a, h_b, w_a, w_b):
    j = pl.program_id(1)

    @pl.when(j == 0)
    def _():
        o_ref[...] = jnp.zeros_like(o_ref)
        h_b[...] = jnp.zeros_like(h_b)
        w_b[...] = jnp.zeros_like(w_b)

    args = (xt_ref, ut_ref, v_ref, c1_ref, p1_ref, s2_ref, p2_ref, o_ref)

    @pl.when(j % 2 == 0)
    def _():
        _peer_step(*args, h_a, h_b, w_a, w_b)

    @pl.when(j % 2 == 1)
    def _():
        _peer_step(*args, h_b, h_a, w_b, w_a)


def peer_experts(xt, ut, v, c1, p1, s2, p2):
    last = PE_TILES - 1
    nb = PE_TM // LANES
    gate_e1 = pl.BlockSpec((nb, PE_E1, PEER_HEADS, LANES), lambda i, j: (i, jnp.clip(j - 1, 0, last), 0, 0))
    gate_e2 = pl.BlockSpec((nb, PEER_HEADS, PEER_N_KEYS, LANES), lambda i, j: (i, 0, 0, 0))
    return pl.pallas_call(
        _peer_kernel,
        grid=(TOKENS // PE_TM, PE_STEPS),
        in_specs=[
            pl.BlockSpec((D_MODEL, PE_TM), lambda i, j: (0, i)),
            pl.BlockSpec((PE_NCH, D_MODEL, PE_CHUNK), lambda i, j: (jnp.minimum(j, last), 0, 0)),
            pl.BlockSpec((PE_NCH, PE_TE, PE_DROWS), lambda i, j: (jnp.clip(j - 2, 0, last), 0, 0)),
            gate_e1, gate_e1, gate_e2, gate_e2,
        ],
        out_specs=pl.BlockSpec((D_MODEL, PE_TM), lambda i, j: (0, i)),
        out_shape=jax.ShapeDtypeStruct((D_MODEL, TOKENS), F32),
        scratch_shapes=[pltpu.VMEM((PE_TE, PE_TM), F32), pltpu.VMEM((PE_TE, PE_TM), F32),
                        pltpu.VMEM((PE_TE, PE_TM), BF16), pltpu.VMEM((PE_TE, PE_TM), BF16)],
        compiler_params=_params(("parallel", "arbitrary")),
        name="peer_experts",
    )(xt, ut, v, c1, p1, s2, p2)


def _res_ln_kernel(x_ref, ft_ref, g_ref, b_ref, y_ref):
    y_ref[...] = _layer_norm(ALPHA * x_ref[...] + ft_ref[...].T, g_ref[...], b_ref[...])


def res_ln(x2d, ft, g, b):
    return pl.pallas_call(
        _res_ln_kernel,
        grid=(TOKENS // LN_TM,),
        in_specs=[
            pl.BlockSpec((LN_TM, D_MODEL), lambda i: (i, 0)),
            pl.BlockSpec((D_MODEL, LN_TM), lambda i: (0, i)),
            pl.BlockSpec((1, D_MODEL), lambda i: (0, 0)),
            pl.BlockSpec((1, D_MODEL), lambda i: (0, 0)),
        ],
        out_specs=pl.BlockSpec((LN_TM, D_MODEL), lambda i: (i, 0)),
        out_shape=jax.ShapeDtypeStruct((TOKENS, D_MODEL), F32),
        compiler_params=_params(("parallel",)),
        name="res_ln",
    )(x2d, ft, g, b)


def _t5_bucket(rel):
    half = T5_BUCKETS // 2
    base = jnp.where(rel > 0, half, 0)
    n = jnp.abs(rel)
    nf = jnp.maximum(n, 1).astype(F32)
    large = T5_MAX_EXACT + (jnp.log(nf / T5_MAX_EXACT) / math.log(T5_MAX_DIST / T5_MAX_EXACT)
                            * (half - T5_MAX_EXACT)).astype(jnp.int32)
    large = jnp.minimum(large, half - 1)
    return base + jnp.where(n < T5_MAX_EXACT, n, large)


def _t5_tiles(t5_bias):
    r = jnp.arange(LANES, dtype=jnp.int32)
    blk = jnp.arange(-2, 3, dtype=jnp.int32)
    rel = blk[:, None, None] * LANES + r[None, None, :] - r[None, :, None]
    onehot = (_t5_bucket(rel)[..., None] == jnp.arange(T5_BUCKETS, dtype=jnp.int32)).astype(F32)
    return jnp.einsum('tqkc,ch->htqk', onehot, t5_bias.astype(F32), precision=lax.Precision.HIGHEST)


def _na_tiles(rpb):
    c = jnp.arange(GRID_W, dtype=jnp.int32)
    col_start = jnp.clip(c - NA_WIN_W // 2, 0, GRID_W - NA_WIN_W)
    in_win = (c[None, :] >= col_start[:, None]) & (c[None, :] < col_start[:, None] + NA_WIN_W)
    dc = jnp.clip(c[None, :] - c[:, None] + (NA_WIN_W - 1), 0, 2 * NA_WIN_W - 2)
    onehot = (dc[..., None] == jnp.arange(2 * NA_WIN_W - 1, dtype=jnp.int32)).astype(F32)
    rows = jnp.stack([rpb[:, NA_WIN_H - 1 - off:2 * NA_WIN_H - 1 - off, :] for off in range(NA_WIN_H)],
                     axis=1).astype(F32)
    bias = jnp.einsum('hoic,qkc->hoqik', rows, onehot, precision=lax.Precision.HIGHEST)
    bias = jnp.where(in_win[None, None, :, None, :], bias, NEG_INF)
    bias = bias.reshape(N_NA_HEADS // 2, 2, NA_WIN_H, GRID_W, NA_KEYS)
    return bias.transpose(0, 2, 1, 3, 4).reshape(N_NA_HEADS // 2, NA_WIN_H, 2 * GRID_W, NA_KEYS)


def kernel(x, w_in, w_out, lam_q1, lam_k1, lam_q2, lam_k2, diff_norm_g, t5_bias, na_rpb,
           ln1_g, ln1_b, peer_wq, peer_keys, peer_u, peer_v, ln2_g, ln2_b):
    xf = x.reshape(TOKENS, D_MODEL)
    t5_tab = _t5_tiles(t5_bias) * LOG2E
    eye = jnp.eye(PEER_HEADS, dtype=F32)
    for l in range(DEPTH):
        lam_init = 0.8 - 0.6 * math.exp(-0.3 * l)
        lam = (jnp.exp(jnp.sum(lam_q1[l] * lam_k1[l])) - jnp.exp(jnp.sum(lam_q2[l] * lam_k2[l]))
               + lam_init).reshape(1).astype(F32)
        proj = in_proj(xf, w_in[l].astype(BF16)).reshape(BATCH, SEQ, IN_PROJ_W)
        g2 = jnp.tile(diff_norm_g[l], 2).reshape(1, LANES)
        o_diff = diff_attention(proj, lam, t5_tab, g2, lam_init)
        o_na = na_attention(proj, _na_tiles(na_rpb[l]))
        x1, x1t = out_proj_ln(o_diff.reshape(TOKENS, -1), o_na.reshape(TOKENS, -1), xf,
                              w_out[l].astype(BF16), ln1_g[l].reshape(1, -1), ln1_b[l].reshape(1, -1))
        wqt = (peer_wq[l].reshape(D_MODEL, PEER_HEADS, 2, PEER_N_KEYS)
               .transpose(2, 1, 3, 0).reshape(2 * EH_ROWS, D_MODEL).astype(BF16))
        keh = jnp.einsum('aed,hg->aehgd', peer_keys[l], eye).reshape(2, EH_ROWS, EH_ROWS).astype(BF16)
        c1, p1, s2, p2 = peer_route(x1t, wqt, keh)
        ut = (peer_u[l].astype(BF16).reshape(PEER_N_EXPERTS // PE_CHUNK, PE_CHUNK, D_MODEL)
              .transpose(0, 2, 1))
        v = (peer_v[l].astype(BF16).reshape(PE_TILES, PE_TE, PE_NCH, PE_DROWS)
             .transpose(0, 2, 1, 3).reshape(PE_TILES * PE_NCH, PE_TE, PE_DROWS))
        ft = peer_experts(x1t, ut, v, c1, p1, s2, p2)
        xf = res_ln(x1, ft, ln2_g[l].reshape(1, -1), ln2_b[l].reshape(1, -1))
    return xf.reshape(BATCH, SEQ, D_MODEL)
```

```python
import functools
import math

import jax
import jax.numpy as jnp
from jax import lax
from jax.experimental import pallas as pl
from jax.experimental.pallas import tpu as pltpu

F32 = jnp.float32
BF16 = jnp.bfloat16

D_MODEL = 1024
BATCH = 8
SEQ = 2048
TOKENS = BATCH * SEQ
DEPTH = 2
HEAD_DIM = 64
N_DIFF_HEADS = 8
DIFF_QK_DIM = 32
N_NA_HEADS = 8
IN_PROJ_W = 3072
GRID_W = 64
GRID_ROWS = SEQ // GRID_W
NA_WIN_H = 8
NA_WIN_W = 16
NA_KEYS = NA_WIN_H * GRID_W
T5_BUCKETS = 32
T5_MAX_EXACT = 8
T5_MAX_DIST = 128
PEER_HEADS = 8
PEER_N_KEYS = 128
PEER_N_EXPERTS = PEER_N_KEYS * PEER_N_KEYS
PEER_TOPK = 16
LN_EPS = 1e-5
NEG_INF = -1e30
ALPHA = (2 * DEPTH) ** 0.25
LOG2E = 1.4426950408889634

LANES = 128
VMEM_LIMIT = 56 * 1024 * 1024

DQ_BLK, DK_BLK, DV_BLK, NQ_BLK, NK_BLK, NV_BLK = 0, 4, 8, 12, 16, 20

STAIR = [(i, j) for i in range(PEER_TOPK) for j in range(PEER_TOPK)
         if (i + 1) * (j + 1) <= PEER_TOPK]

NT_DIMS = (((1,), (1,)), ((), ()))
TN_DIMS = (((0,), (0,)), ((), ()))


def _params(sem, vmem=VMEM_LIMIT, flags=None):
    return pltpu.CompilerParams(dimension_semantics=sem, vmem_limit_bytes=vmem, flags=flags)


IN_TM = 512


def _in_proj_kernel(x_ref, w_ref, o_ref):
    o_ref[...] = jnp.dot(x_ref[...].astype(BF16), w_ref[...],
                         preferred_element_type=F32).astype(BF16)


def in_proj(x2d, w):
    return pl.pallas_call(
        _in_proj_kernel,
        grid=(TOKENS // IN_TM,),
        in_specs=[pl.BlockSpec((IN_TM, D_MODEL), lambda i: (i, 0)),
                  pl.BlockSpec((D_MODEL, IN_PROJ_W), lambda i: (0, 0))],
        out_specs=pl.BlockSpec((IN_TM, IN_PROJ_W), lambda i: (i, 0)),
        out_shape=jax.ShapeDtypeStruct((TOKENS, IN_PROJ_W), BF16),
        compiler_params=_params(("parallel",)),
        name="in_proj",
    )(x2d, w)


DIFF_TQ = 512
KEY_BLOCKS = SEQ // LANES


def _diff_kernel(lam_ref, q_ref, k_ref, v_ref, tab_ref, g_ref, o_ref, *, lam_init):
    qi = pl.program_id(2)
    lam = lam_ref[0]
    q = q_ref[0].astype(F32) * (DIFF_QK_DIM ** -0.5 * LOG2E)
    k = k_ref[0]
    v = v_ref[0]
    lane = lax.broadcasted_iota(jnp.int32, (1, LANES), 1)

    def scores(hh):
        rows = []
        for sb in range(DIFF_TQ // LANES):
            qblk = qi * (DIFF_TQ // LANES) + sb
            tiles = [tab_ref[hh, jnp.clip(kj - qblk, -2, 2) + 2] for kj in range(KEY_BLOCKS)]
            rows.append(jnp.concatenate(tiles, axis=1))
        bias = jnp.concatenate(rows, axis=0)
        out = []
        for m in range(2):
            c0 = hh * HEAD_DIM + m * DIFF_QK_DIM
            msk = (lane >= c0) & (lane < c0 + DIFF_QK_DIM)
            qs = jnp.where(msk, q, 0.0).astype(BF16)
            out.append(lax.dot_general(qs, k, NT_DIMS, preferred_element_type=F32) + bias)
        return out

    def weights(ss):
        es = [jnp.exp2(s - jnp.max(s, axis=-1, keepdims=True)) for s in ss]
        zs = [jnp.sum(e, axis=-1, keepdims=True) for e in es]
        return (es[0] - (lam * zs[0] / zs[1]) * es[1]).astype(BF16), 1.0 / zs[0]

    def head_out(hh, w, rz):
        o = jnp.dot(w, v, preferred_element_type=F32) * rz
        hm = (lane >= hh * HEAD_DIM) & (lane < (hh + 1) * HEAD_DIM)
        ms = jnp.sum(jnp.where(hm, o * o, 0.0), axis=-1, keepdims=True) * (1.0 / HEAD_DIM)
        return jnp.where(hm, o * lax.rsqrt(ms + LN_EPS), 0.0)

    s0 = scores(0)
    s1 = scores(1)
    w0, rz0 = weights(s0)
    o0 = head_out(0, w0, rz0)
    w1, rz1 = weights(s1)
    o1 = head_out(1, w1, rz1)
    o_ref[0] = ((o0 + o1) * g_ref[...] * (1.0 - lam_init)).astype(BF16)


def diff_attention(proj, lam, t5_tab, norm_g2, lam_init):
    nq = SEQ // DIFF_TQ
    return pl.pallas_call(
        functools.partial(_diff_kernel, lam_init=lam_init),
        grid=(BATCH, N_DIFF_HEADS // 2, nq),
        in_specs=[
            pl.BlockSpec(memory_space=pltpu.SMEM),
            pl.BlockSpec((1, DIFF_TQ, LANES), lambda b, h, i: (b, i, DQ_BLK + h)),
            pl.BlockSpec((1, SEQ, LANES), lambda b, h, i: (b, 0, DK_BLK + h)),
            pl.BlockSpec((1, SEQ, LANES), lambda b, h, i: (b, 0, DV_BLK + h)),
            pl.BlockSpec((2, 5, LANES, LANES), lambda b, h, i: (h, 0, 0, 0)),
            pl.BlockSpec((1, LANES), lambda b, h, i: (0, 0)),
        ],
        out_specs=pl.BlockSpec((1, DIFF_TQ, LANES), lambda b, h, i: (b, i, h)),
        out_shape=jax.ShapeDtypeStruct((BATCH, SEQ, N_DIFF_HEADS * HEAD_DIM), BF16),
        compiler_params=_params(("parallel", "parallel", "arbitrary")),
        name="diff_attention",
    )(lam, proj, proj, proj, t5_tab, norm_g2)


def _na_kernel(q_ref, k_ref, v_ref, tab_ref, o_ref):
    lane = lax.broadcasted_iota(jnp.int32, (1, LANES), 1)
    h0 = lane < HEAD_DIM
    for r in range(GRID_ROWS):
        rs = min(max(r - NA_WIN_H // 2, 0), GRID_ROWS - NA_WIN_H)
        qr = q_ref[0, r * GRID_W:(r + 1) * GRID_W, :].astype(F32) * (HEAD_DIM ** -0.5)
        kw = k_ref[0, rs * GRID_W:rs * GRID_W + NA_KEYS, :]
        vw = v_ref[0, rs * GRID_W:rs * GRID_W + NA_KEYS, :]
        q2 = jnp.concatenate([jnp.where(h0, qr, 0.0), jnp.where(h0, 0.0, qr)], axis=0).astype(BF16)
        s = lax.dot_general(q2, kw, NT_DIMS, preferred_element_type=F32) + tab_ref[0, r - rs]
        e = jnp.exp(s - jnp.max(s, axis=-1, keepdims=True))
        p = e * (1.0 / jnp.sum(e, axis=-1, keepdims=True))
        o = jnp.dot(p.astype(BF16), vw, preferred_element_type=F32)
        o_ref[0, r * GRID_W:(r + 1) * GRID_W, :] = jnp.where(h0, o[:GRID_W], o[GRID_W:]).astype(BF16)


def na_attention(proj, na_tab):
    return pl.pallas_call(
        _na_kernel,
        grid=(BATCH, N_NA_HEADS // 2),
        in_specs=[
            pl.BlockSpec((1, SEQ, LANES), lambda b, h: (b, 0, NQ_BLK + h)),
            pl.BlockSpec((1, SEQ, LANES), lambda b, h: (b, 0, NK_BLK + h)),
            pl.BlockSpec((1, SEQ, LANES), lambda b, h: (b, 0, NV_BLK + h)),
            pl.BlockSpec((1, NA_WIN_H, 2 * GRID_W, NA_KEYS), lambda b, h: (h, 0, 0, 0)),
        ],
        out_specs=pl.BlockSpec((1, SEQ, LANES), lambda b, h: (b, 0, h)),
        out_shape=jax.ShapeDtypeStruct((BATCH, SEQ, N_NA_HEADS * HEAD_DIM), BF16),
        compiler_params=_params(("parallel", "parallel")),
        name="na_attention",
    )(proj, proj, proj, na_tab)


LN_TM = 512


def _layer_norm(y, g, b):
    mu = jnp.mean(y, axis=-1, keepdims=True)
    yc = y - mu
    var = jnp.mean(yc * yc, axis=-1, keepdims=True)
    return yc * lax.rsqrt(var + LN_EPS) * g + b


def _out_ln_kernel(od_ref, on_ref, x_ref, w_ref, g_ref, b_ref, y_ref, yt_ref):
    half = N_DIFF_HEADS * HEAD_DIM
    h = jnp.dot(od_ref[...], w_ref[0:half, :], preferred_element_type=F32)
    h = h + jnp.dot(on_ref[...], w_ref[half:2 * half, :], preferred_element_type=F32)
    y = _layer_norm(ALPHA * x_ref[...] + h, g_ref[...], b_ref[...])
    y_ref[...] = y
    yt_ref[...] = y.T.astype(BF16)


def out_proj_ln(o_diff, o_na, x2d, w, g, b):
    half = N_DIFF_HEADS * HEAD_DIM
    return pl.pallas_call(
        _out_ln_kernel,
        grid=(TOKENS // LN_TM,),
        in_specs=[
            pl.BlockSpec((LN_TM, half), lambda i: (i, 0)),
            pl.BlockSpec((LN_TM, half), lambda i: (i, 0)),
            pl.BlockSpec((LN_TM, D_MODEL), lambda i: (i, 0)),
            pl.BlockSpec((D_MODEL, D_MODEL), lambda i: (0, 0)),
            pl.BlockSpec((1, D_MODEL), lambda i: (0, 0)),
            pl.BlockSpec((1, D_MODEL), lambda i: (0, 0)),
        ],
        out_specs=[pl.BlockSpec((LN_TM, D_MODEL), lambda i: (i, 0)),
                   pl.BlockSpec((D_MODEL, LN_TM), lambda i: (0, i))],
        out_shape=[jax.ShapeDtypeStruct((TOKENS, D_MODEL), F32),
                   jax.ShapeDtypeStruct((D_MODEL, TOKENS), BF16)],
        compiler_params=_params(("parallel",)),
        name="out_proj_ln",
    )(o_diff, o_na, x2d, w, g, b)


RT_TM = 512
EH_ROWS = PEER_N_KEYS * PEER_HEADS


def _cmp_exchange(xs, i, l):
    a, b = xs[i], xs[l]
    xs[i], xs[l] = jnp.maximum(a, b), jnp.minimum(a, b)


def _bitonic_merge_desc(xs):
    xs = list(xs)
    j = len(xs) // 2
    while j >= 1:
        for i in range(len(xs)):
            if i & j == 0:
                _cmp_exchange(xs, i, i | j)
        j //= 2
    return xs


def _sort16_desc(xs):
    xs = list(xs)
    k = 2
    while k <= PEER_TOPK:
        j = k // 2
        while j >= 1:
            for i in range(PEER_TOPK):
                l = i ^ j
                if l > i:
                    if i & k == 0 or k == PEER_TOPK:
                        _cmp_exchange(xs, i, l)
                    else:
                        _cmp_exchange(xs, l, i)
            j //= 2
        k *= 2
    return xs


def _top16_desc(xs):
    runs = [_sort16_desc(xs[g:g + PEER_TOPK]) for g in range(0, len(xs), PEER_TOPK)]
    while len(runs) > 1:
        nxt = []
        for a, b in zip(runs[0::2], runs[1::2]):
            nxt.append(_bitonic_merge_desc([jnp.maximum(a[i], b[PEER_TOPK - 1 - i])
                                            for i in range(PEER_TOPK)]))
        runs = nxt
    return runs[0]


BIG = 3.0e38


def _route_kernel(xt_ref, wqt_ref, keh_ref, c1_ref, p1_ref, s2_ref, p2_ref, s_scr, t_scr):
    qt = jnp.dot(wqt_ref[...], xt_ref[...], preferred_element_type=F32).astype(BF16)
    for a in range(2):
        s = jnp.dot(keh_ref[a], qt[a * EH_ROWS:(a + 1) * EH_ROWS], preferred_element_type=F32)
        for b in range(RT_TM // LANES):
            s_scr[a, b] = s[:, b * LANES:(b + 1) * LANES]

    def block(b, carry):
        s1 = s_scr[0, b].reshape(PEER_N_KEYS, PEER_HEADS, LANES)
        s2 = s_scr[1, b].reshape(PEER_N_KEYS, PEER_HEADS, LANES)
        v1 = _top16_desc([s1[e] for e in range(PEER_N_KEYS)])
        v2 = _top16_desc([s2[e] for e in range(PEER_N_KEYS)])
        cands = [v1[i] + v2[j] for (i, j) in STAIR]
        pad = [jnp.full_like(cands[0], -jnp.inf)] * (-len(cands) % PEER_TOPK)
        best = _top16_desc(cands + pad)
        top, thr = best[0], best[PEER_TOPK - 1]
        z = jnp.zeros_like(top)
        cut = [jnp.full_like(top, BIG) for _ in range(PEER_TOPK)]
        for (i, j), c in zip(STAIR, cands):
            sel = c >= thr
            z = z + jnp.where(sel, jnp.exp(c - top), 0.0)
            cut[i] = jnp.where(sel, jnp.minimum(cut[i], v2[j]), cut[i])
        inv_z = 1.0 / z
        c1 = jnp.full_like(s1, BIG)
        for i in range(PEER_TOPK):
            c1 = jnp.where(s1 == v1[i][None], cut[i][None], c1)
        c1_ref[b] = c1
        p1_ref[b] = jnp.exp(s1 - v1[0][None]) * inv_z[None]
        t_scr[...] = jnp.exp(s2 - v2[0][None]).reshape(EH_ROWS, LANES)
        for h in range(PEER_HEADS):
            s2_ref[b, h] = s_scr[1, b, pl.ds(h, PEER_N_KEYS, stride=PEER_HEADS), :]
            p2_ref[b, h] = t_scr[pl.ds(h, PEER_N_KEYS, stride=PEER_HEADS), :]
        return carry

    lax.fori_loop(0, RT_TM // LANES, block, 0)


def peer_route(xt, wqt, keh):
    nblk = TOKENS // LANES
    nb = RT_TM // LANES
    shp_kh = jax.ShapeDtypeStruct((nblk, PEER_N_KEYS, PEER_HEADS, LANES), F32)
    shp_hk = jax.ShapeDtypeStruct((nblk, PEER_HEADS, PEER_N_KEYS, LANES), F32)
    ospec_kh = pl.BlockSpec((nb, PEER_N_KEYS, PEER_HEADS, LANES), lambda i: (i, 0, 0, 0))
    ospec_hk = pl.BlockSpec((nb, PEER_HEADS, PEER_N_KEYS, LANES), lambda i: (i, 0, 0, 0))
    return pl.pallas_call(
        _route_kernel,
        grid=(TOKENS // RT_TM,),
        in_specs=[
            pl.BlockSpec((D_MODEL, RT_TM), lambda i: (0, i)),
            pl.BlockSpec((2 * EH_ROWS, D_MODEL), lambda i: (0, 0)),
            pl.BlockSpec((2, EH_ROWS, EH_ROWS), lambda i: (0, 0, 0)),
        ],
        out_specs=[ospec_kh, ospec_kh, ospec_hk, ospec_hk],
        out_shape=[shp_kh, shp_kh, shp_hk, shp_hk],
        scratch_shapes=[pltpu.VMEM((2, nb, EH_ROWS, LANES), F32), pltpu.VMEM((EH_ROWS, LANES), F32)],
        compiler_params=_params(("parallel",)),
        name="peer_route",
    )(xt, wqt, keh)


PE_TM = 512
PE_E1 = 8
PE_TE = PE_E1 * PEER_N_KEYS


def _gelu(x):
    return x * (lax.erf(x * (1.0 / math.sqrt(2.0))) + 1.0) * 0.5


PE_TILES = PEER_N_KEYS // PE_E1
PE_E1G = 2
PE_CHUNK = PE_E1G * PEER_N_KEYS
PE_NCH = PE_TE // PE_CHUNK
PE_DROWS = D_MODEL // PE_NCH
PE_K2 = 64
PE_LAG3 = 2


def _peer_chunk(c, xt_ref, ut_ref, v_ref, c1_ref, p1_ref, s2_ref, p2_ref, o_ref, h_new, h_old, w_new, w_old):
    c0 = pl.multiple_of(c * PE_CHUNK, PE_CHUNK)
    d0 = pl.multiple_of(c * PE_DROWS, PE_DROWS)
    for tc in range(PE_TM // LANES):
        for k0 in range(0, PEER_N_KEYS, PE_K2):
            g = [None] * PE_E1G
            for h in range(PEER_HEADS):
                s2 = s2_ref[tc, h, pl.ds(k0, PE_K2), :]
                p2 = p2_ref[tc, h, pl.ds(k0, PE_K2), :]
                for e in range(PE_E1G):
                    el = c * PE_E1G + e
                    c1 = c1_ref[tc, el, pl.ds(h, 1), :]
                    p1 = p1_ref[tc, el, pl.ds(h, 1), :]
                    t = jnp.where(s2 >= c1, p1 * p2, 0.0)
                    g[e] = t if g[e] is None else g[e] + t
            for e in range(PE_E1G):
                rows = pl.ds(c0 + e * PEER_N_KEYS + k0, PE_K2)
                ts = pl.ds(tc * LANES, LANES)
                w_new[rows, ts] = (g[e] * _gelu(h_old[rows, ts])).astype(BF16)
    h_new[pl.ds(c0, PE_CHUNK), :] = lax.dot_general(ut_ref[c], xt_ref[...], TN_DIMS,
                                                    preferred_element_type=F32)
    o_ref[pl.ds(d0, PE_DROWS), :] += lax.dot_general(v_ref[c], w_old[...], TN_DIMS,
                                                     preferred_element_type=F32)


def _peer_step(*refs):
    def body(c, carry):
        _peer_chunk(c, *refs)
        return carry

    lax.fori_loop(0, PE_NCH, body, 0)


def _peer_kernel(xt_ref, ut_ref, v_ref, c1_ref, p1_ref, s2_ref, p2_ref, o_ref, h_a, h_b, w_a, w_b):
    s = pl.program_id(0)

    @pl.when(s == 0)
    def _():
        h_b[...] = jnp.zeros_like(h_b)
        w_b[...] = jnp.zeros_like(w_b)

    @pl.when((s == 0) | (s % PE_TILES == PE_LAG3))
    def _():
        o_ref[...] = jnp.zeros_like(o_ref)

    args = (xt_ref, ut_ref, v_ref, c1_ref, p1_ref, s2_ref, p2_ref, o_ref)

    @pl.when(s % 2 == 0)
    def _():
        _peer_step(*args, h_a, h_b, w_a, w_b)

    @pl.when(s % 2 == 1)
    def _():
        _peer_step(*args, h_b, h_a, w_b, w_a)


def peer_experts(xt, ut, v, c1, p1, s2, p2):
    n_items = (TOKENS // PE_TM) * PE_TILES
    nb = PE_TM // LANES

    def item(s, lag):
        w = jnp.clip(s - lag, 0, n_items - 1)
        return w // PE_TILES, w % PE_TILES

    gate_e1 = pl.BlockSpec((nb, PE_E1, PEER_HEADS, LANES), lambda s: (*item(s, 1), 0, 0))
    gate_e2 = pl.BlockSpec((nb, PEER_HEADS, PEER_N_KEYS, LANES), lambda s: (item(s, 1)[0], 0, 0, 0))
    return pl.pallas_call(
        _peer_kernel,
        grid=(n_items + PE_LAG3,),
        in_specs=[
            pl.BlockSpec((D_MODEL, PE_TM), lambda s: (0, item(s, 0)[0])),
            pl.BlockSpec((PE_NCH, D_MODEL, PE_CHUNK), lambda s: (item(s, 0)[1], 0, 0)),
            pl.BlockSpec((PE_NCH, PE_TE, PE_DROWS), lambda s: (item(s, PE_LAG3)[1], 0, 0)),
            gate_e1, gate_e1, gate_e2, gate_e2,
        ],
        out_specs=pl.BlockSpec((D_MODEL, PE_TM), lambda s: (0, item(s, PE_LAG3)[0])),
        out_shape=jax.ShapeDtypeStruct((D_MODEL, TOKENS), F32),
        scratch_shapes=[pltpu.VMEM((PE_TE, PE_TM), F32), pltpu.VMEM((PE_TE, PE_TM), F32),
                        pltpu.VMEM((PE_TE, PE_TM), BF16), pltpu.VMEM((PE_TE, PE_TM), BF16)],
        compiler_params=_params(("arbitrary",)),
        name="peer_experts",
    )(xt, ut, v, c1, p1, s2, p2)


def _res_ln_kernel(x_ref, ft_ref, g_ref, b_ref, y_ref):
    y_ref[...] = _layer_norm(ALPHA * x_ref[...] + ft_ref[...].T, g_ref[...], b_ref[...])


def res_ln(x2d, ft, g, b):
    return pl.pallas_call(
        _res_ln_kernel,
        grid=(TOKENS // LN_TM,),
        in_specs=[
            pl.BlockSpec((LN_TM, D_MODEL), lambda i: (i, 0)),
            pl.BlockSpec((D_MODEL, LN_TM), lambda i: (0, i)),
            pl.BlockSpec((1, D_MODEL), lambda i: (0, 0)),
            pl.BlockSpec((1, D_MODEL), lambda i: (0, 0)),
        ],
        out_specs=pl.BlockSpec((LN_TM, D_MODEL), lambda i: (i, 0)),
        out_shape=jax.ShapeDtypeStruct((TOKENS, D_MODEL), F32),
        compiler_params=_params(("parallel",)),
        name="res_ln",
    )(x2d, ft, g, b)


def _t5_bucket(rel):
    half = T5_BUCKETS // 2
    base = jnp.where(rel > 0, half, 0)
    n = jnp.abs(rel)
    nf = jnp.maximum(n, 1).astype(F32)
    large = T5_MAX_EXACT + (jnp.log(nf / T5_MAX_EXACT) / math.log(T5_MAX_DIST / T5_MAX_EXACT)
                            * (half - T5_MAX_EXACT)).astype(jnp.int32)
    large = jnp.minimum(large, half - 1)
    return base + jnp.where(n < T5_MAX_EXACT, n, large)


def _t5_tiles(t5_bias):
    r = jnp.arange(LANES, dtype=jnp.int32)
    blk = jnp.arange(-2, 3, dtype=jnp.int32)
    rel = blk[:, None, None] * LANES + r[None, None, :] - r[None, :, None]
    onehot = (_t5_bucket(rel)[..., None] == jnp.arange(T5_BUCKETS, dtype=jnp.int32)).astype(F32)
    return jnp.einsum('tqkc,ch->htqk', onehot, t5_bias.astype(F32), precision=lax.Precision.HIGHEST)


def _na_tiles(rpb):
    c = jnp.arange(GRID_W, dtype=jnp.int32)
    col_start = jnp.clip(c - NA_WIN_W // 2, 0, GRID_W - NA_WIN_W)
    in_win = (c[None, :] >= col_start[:, None]) & (c[None, :] < col_start[:, None] + NA_WIN_W)
    dc = jnp.clip(c[None, :] - c[:, None] + (NA_WIN_W - 1), 0, 2 * NA_WIN_W - 2)
    onehot = (dc[..., None] == jnp.arange(2 * NA_WIN_W - 1, dtype=jnp.int32)).astype(F32)
    rows = jnp.stack([rpb[:, NA_WIN_H - 1 - off:2 * NA_WIN_H - 1 - off, :] for off in range(NA_WIN_H)],
                     axis=1).astype(F32)
    bias = jnp.einsum('hoic,qkc->hoqik', rows, onehot, precision=lax.Precision.HIGHEST)
    bias = jnp.where(in_win[None, None, :, None, :], bias, NEG_INF)
    bias = bias.reshape(N_NA_HEADS // 2, 2, NA_WIN_H, GRID_W, NA_KEYS)
    return bias.transpose(0, 2, 1, 3, 4).reshape(N_NA_HEADS // 2, NA_WIN_H, 2 * GRID_W, NA_KEYS)


def kernel(x, w_in, w_out, lam_q1, lam_k1, lam_q2, lam_k2, diff_norm_g, t5_bias, na_rpb,
           ln1_g, ln1_b, peer_wq, peer_keys, peer_u, peer_v, ln2_g, ln2_b):
    xf = x.reshape(TOKENS, D_MODEL)
    t5_tab = _t5_tiles(t5_bias) * LOG2E
    eye = jnp.eye(PEER_HEADS, dtype=F32)
    for l in range(DEPTH):
        lam_init = 0.8 - 0.6 * math.exp(-0.3 * l)
        lam = (jnp.exp(jnp.sum(lam_q1[l] * lam_k1[l])) - jnp.exp(jnp.sum(lam_q2[l] * lam_k2[l]))
               + lam_init).reshape(1).astype(F32)
        proj = in_proj(xf, w_in[l].astype(BF16)).reshape(BATCH, SEQ, IN_PROJ_W)
        g2 = jnp.tile(diff_norm_g[l], 2).reshape(1, LANES)
        o_diff = diff_attention(proj, lam, t5_tab, g2, lam_init)
        o_na = na_attention(proj, _na_tiles(na_rpb[l]))
        x1, x1t = out_proj_ln(o_diff.reshape(TOKENS, -1), o_na.reshape(TOKENS, -1), xf,
                              w_out[l].astype(BF16), ln1_g[l].reshape(1, -1), ln1_b[l].reshape(1, -1))
        wqt = (peer_wq[l].reshape(D_MODEL, PEER_HEADS, 2, PEER_N_KEYS)
               .transpose(2, 1, 3, 0).reshape(2 * EH_ROWS, D_MODEL).astype(BF16))
        keh = jnp.einsum('aed,hg->aehgd', peer_keys[l], eye).reshape(2, EH_ROWS, EH_ROWS).astype(BF16)
        c1, p1, s2, p2 = peer_route(x1t, wqt, keh)
        ut = (peer_u[l].reshape(PEER_N_EXPERTS // PE_CHUNK, PE_CHUNK, D_MODEL)
              .transpose(0, 2, 1).astype(BF16))
        v = (peer_v[l].reshape(PE_TILES, PE_TE, PE_NCH, PE_DROWS)
             .transpose(0, 2, 1, 3).astype(BF16).reshape(PE_TILES * PE_NCH, PE_TE, PE_DROWS))
        ft = peer_experts(x1t, ut, v, c1, p1, s2, p2)
        xf = res_ln(x1, ft, ln2_g[l].reshape(1, -1), ln2_b[l].reshape(1, -1))
    return xf.reshape(BATCH, SEQ, D_MODEL)
```

```python
import functools
import math

import jax
import jax.numpy as jnp
from jax import lax
from jax.experimental import pallas as pl
from jax.experimental.pallas import tpu as pltpu

F32 = jnp.float32
BF16 = jnp.bfloat16

D_MODEL = 1024
BATCH = 8
SEQ = 2048
TOKENS = BATCH * SEQ
DEPTH = 2
HEAD_DIM = 64
N_DIFF_HEADS = 8
DIFF_QK_DIM = 32
N_NA_HEADS = 8
IN_PROJ_W = 3072
GRID_W = 64
GRID_ROWS = SEQ // GRID_W
NA_WIN_H = 8
NA_WIN_W = 16
NA_KEYS = NA_WIN_H * GRID_W
T5_BUCKETS = 32
T5_MAX_EXACT = 8
T5_MAX_DIST = 128
PEER_HEADS = 8
PEER_N_KEYS = 128
PEER_N_EXPERTS = PEER_N_KEYS * PEER_N_KEYS
PEER_TOPK = 16
LN_EPS = 1e-5
NEG_INF = -1e30
ALPHA = (2 * DEPTH) ** 0.25
LOG2E = 1.4426950408889634

LANES = 128
VMEM_LIMIT = 56 * 1024 * 1024

DQ_BLK, DK_BLK, DV_BLK, NQ_BLK, NK_BLK, NV_BLK = 0, 4, 8, 12, 16, 20

STAIR = [(i, j) for i in range(PEER_TOPK) for j in range(PEER_TOPK)
         if (i + 1) * (j + 1) <= PEER_TOPK]

NT_DIMS = (((1,), (1,)), ((), ()))
TN_DIMS = (((0,), (0,)), ((), ()))


def _params(sem, vmem=VMEM_LIMIT, flags=None):
    return pltpu.CompilerParams(dimension_semantics=sem, vmem_limit_bytes=vmem, flags=flags)


IN_TM = 512


def _in_proj_kernel(x_ref, w_ref, o_ref):
    o_ref[...] = jnp.dot(x_ref[...].astype(BF16), w_ref[...],
                         preferred_element_type=F32).astype(BF16)


def in_proj(x2d, w):
    return pl.pallas_call(
        _in_proj_kernel,
        grid=(TOKENS // IN_TM,),
        in_specs=[pl.BlockSpec((IN_TM, D_MODEL), lambda i: (i, 0)),
                  pl.BlockSpec((D_MODEL, IN_PROJ_W), lambda i: (0, 0))],
        out_specs=pl.BlockSpec((IN_TM, IN_PROJ_W), lambda i: (i, 0)),
        out_shape=jax.ShapeDtypeStruct((TOKENS, IN_PROJ_W), BF16),
        compiler_params=_params(("parallel",)),
        name="in_proj",
    )(x2d, w)


DIFF_TQ = 512
KEY_BLOCKS = SEQ // LANES


def _diff_kernel(lam_ref, q_ref, k_ref, v_ref, tab_ref, g_ref, o_ref, *, lam_init):
    qi = pl.program_id(2)
    lam = lam_ref[0]
    q = q_ref[0].astype(F32) * (DIFF_QK_DIM ** -0.5 * LOG2E)
    k = k_ref[0]
    v = v_ref[0]
    lane = lax.broadcasted_iota(jnp.int32, (1, LANES), 1)

    def scores(hh):
        rows = []
        for sb in range(DIFF_TQ // LANES):
            qblk = qi * (DIFF_TQ // LANES) + sb
            tiles = [tab_ref[hh, jnp.clip(kj - qblk, -2, 2) + 2] for kj in range(KEY_BLOCKS)]
            rows.append(jnp.concatenate(tiles, axis=1))
        bias = jnp.concatenate(rows, axis=0)
        out = []
        for m in range(2):
            c0 = hh * HEAD_DIM + m * DIFF_QK_DIM
            msk = (lane >= c0) & (lane < c0 + DIFF_QK_DIM)
            qs = jnp.where(msk, q, 0.0).astype(BF16)
            out.append(lax.dot_general(qs, k, NT_DIMS, preferred_element_type=F32) + bias)
        return out

    def weights(ss):
        es = [jnp.exp2(s - jnp.max(s, axis=-1, keepdims=True)) for s in ss]
        zs = [jnp.sum(e, axis=-1, keepdims=True) for e in es]
        return (es[0] - (lam * zs[0] / zs[1]) * es[1]).astype(BF16), 1.0 / zs[0]

    def head_out(hh, w, rz):
        o = jnp.dot(w, v, preferred_element_type=F32) * rz
        hm = (lane >= hh * HEAD_DIM) & (lane < (hh + 1) * HEAD_DIM)
        ms = jnp.sum(jnp.where(hm, o * o, 0.0), axis=-1, keepdims=True) * (1.0 / HEAD_DIM)
        return jnp.where(hm, o * lax.rsqrt(ms + LN_EPS), 0.0)

    s0 = scores(0)
    s1 = scores(1)
    w0, rz0 = weights(s0)
    o0 = head_out(0, w0, rz0)
    w1, rz1 = weights(s1)
    o1 = head_out(1, w1, rz1)
    o_ref[0] = ((o0 + o1) * g_ref[...] * (1.0 - lam_init)).astype(BF16)


def diff_attention(proj, lam, t5_tab, norm_g2, lam_init):
    nq = SEQ // DIFF_TQ
    return pl.pallas_call(
        functools.partial(_diff_kernel, lam_init=lam_init),
        grid=(BATCH, N_DIFF_HEADS // 2, nq),
        in_specs=[
            pl.BlockSpec(memory_space=pltpu.SMEM),
            pl.BlockSpec((1, DIFF_TQ, LANES), lambda b, h, i: (b, i, DQ_BLK + h)),
            pl.BlockSpec((1, SEQ, LANES), lambda b, h, i: (b, 0, DK_BLK + h)),
            pl.BlockSpec((1, SEQ, LANES), lambda b, h, i: (b, 0, DV_BLK + h)),
            pl.BlockSpec((2, 5, LANES, LANES), lambda b, h, i: (h, 0, 0, 0)),
            pl.BlockSpec((1, LANES), lambda b, h, i: (0, 0)),
        ],
        out_specs=pl.BlockSpec((1, DIFF_TQ, LANES), lambda b, h, i: (b, i, h)),
        out_shape=jax.ShapeDtypeStruct((BATCH, SEQ, N_DIFF_HEADS * HEAD_DIM), BF16),
        compiler_params=_params(("parallel", "parallel", "arbitrary")),
        name="diff_attention",
    )(lam, proj, proj, proj, t5_tab, norm_g2)


NA_AHEAD = 4

def _na_kernel(q_ref, k_ref, v_ref, tab_ref, o_ref):
    lane = lax.broadcasted_iota(jnp.int32, (1, LANES), 1)
    h0 = lane < HEAD_DIM

    def win(r):
        return min(max(r - NA_WIN_H // 2, 0), GRID_ROWS - NA_WIN_H)

    def scores(r):
        rs = win(r)
        qr = q_ref[0, r * GRID_W:(r + 1) * GRID_W, :].astype(F32) * (HEAD_DIM ** -0.5)
        kw = k_ref[0, rs * GRID_W:rs * GRID_W + NA_KEYS, :]
        q2 = jnp.concatenate([jnp.where(h0, qr, 0.0), jnp.where(h0, 0.0, qr)], axis=0).astype(BF16)
        return lax.dot_general(q2, kw, NT_DIMS, preferred_element_type=F32) + tab_ref[0, r - rs]

    pending = [scores(r) for r in range(NA_AHEAD)]
    for r in range(GRID_ROWS):
        s = pending.pop(0)
        if r + NA_AHEAD < GRID_ROWS:
            pending.append(scores(r + NA_AHEAD))
        rs = win(r)
        vw = v_ref[0, rs * GRID_W:rs * GRID_W + NA_KEYS, :]
        e = jnp.exp(s - jnp.max(s, axis=-1, keepdims=True))
        p = e * (1.0 / jnp.sum(e, axis=-1, keepdims=True))
        o = jnp.dot(p.astype(BF16), vw, preferred_element_type=F32)
        o_ref[0, r * GRID_W:(r + 1) * GRID_W, :] = jnp.where(h0, o[:GRID_W], o[GRID_W:]).astype(BF16)


def na_attention(proj, na_tab):
    return pl.pallas_call(
        _na_kernel,
        grid=(BATCH, N_NA_HEADS // 2),
        in_specs=[
            pl.BlockSpec((1, SEQ, LANES), lambda b, h: (b, 0, NQ_BLK + h)),
            pl.BlockSpec((1, SEQ, LANES), lambda b, h: (b, 0, NK_BLK + h)),
            pl.BlockSpec((1, SEQ, LANES), lambda b, h: (b, 0, NV_BLK + h)),
            pl.BlockSpec((1, NA_WIN_H, 2 * GRID_W, NA_KEYS), lambda b, h: (h, 0, 0, 0)),
        ],
        out_specs=pl.BlockSpec((1, SEQ, LANES), lambda b, h: (b, 0, h)),
        out_shape=jax.ShapeDtypeStruct((BATCH, SEQ, N_NA_HEADS * HEAD_DIM), BF16),
        compiler_params=_params(("parallel", "parallel")),
        name="na_attention",
    )(proj, proj, proj, na_tab)


LN_TM = 512


def _layer_norm(y, g, b):
    mu = jnp.mean(y, axis=-1, keepdims=True)
    yc = y - mu
    var = jnp.mean(yc * yc, axis=-1, keepdims=True)
    return yc * lax.rsqrt(var + LN_EPS) * g + b


def _out_ln_kernel(od_ref, on_ref, x_ref, w_ref, g_ref, b_ref, y_ref, yt_ref):
    half = N_DIFF_HEADS * HEAD_DIM
    h = jnp.dot(od_ref[...], w_ref[0:half, :], preferred_element_type=F32)
    h = h + jnp.dot(on_ref[...], w_ref[half:2 * half, :], preferred_element_type=F32)
    y = _layer_norm(ALPHA * x_ref[...] + h, g_ref[...], b_ref[...])
    y_ref[...] = y
    yt_ref[0] = y.T.astype(BF16)


def out_proj_ln(o_diff, o_na, x2d, w, g, b):
    half = N_DIFF_HEADS * HEAD_DIM
    return pl.pallas_call(
        _out_ln_kernel,
        grid=(TOKENS // LN_TM,),
        in_specs=[
            pl.BlockSpec((LN_TM, half), lambda i: (i, 0)),
            pl.BlockSpec((LN_TM, half), lambda i: (i, 0)),
            pl.BlockSpec((LN_TM, D_MODEL), lambda i: (i, 0)),
            pl.BlockSpec((D_MODEL, D_MODEL), lambda i: (0, 0)),
            pl.BlockSpec((1, D_MODEL), lambda i: (0, 0)),
            pl.BlockSpec((1, D_MODEL), lambda i: (0, 0)),
        ],
        out_specs=[pl.BlockSpec((LN_TM, D_MODEL), lambda i: (i, 0)),
                   pl.BlockSpec((1, D_MODEL, LN_TM), lambda i: (i, 0, 0))],
        out_shape=[jax.ShapeDtypeStruct((TOKENS, D_MODEL), F32),
                   jax.ShapeDtypeStruct((TOKENS // LN_TM, D_MODEL, LN_TM), BF16)],
        compiler_params=_params(("parallel",)),
        name="out_proj_ln",
    )(o_diff, o_na, x2d, w, g, b)


RT_TM = LN_TM
EH_ROWS = PEER_N_KEYS * PEER_HEADS


def _cmp_exchange(xs, i, l):
    a, b = xs[i], xs[l]
    xs[i], xs[l] = jnp.maximum(a, b), jnp.minimum(a, b)


def _bitonic_merge_desc(xs):
    xs = list(xs)
    j = len(xs) // 2
    while j >= 1:
        for i in range(len(xs)):
            if i & j == 0:
                _cmp_exchange(xs, i, i | j)
        j //= 2
    return xs


def _sort16_desc(xs):
    xs = list(xs)
    k = 2
    while k <= PEER_TOPK:
        j = k // 2
        while j >= 1:
            for i in range(PEER_TOPK):
                l = i ^ j
                if l > i:
                    if i & k == 0 or k == PEER_TOPK:
                        _cmp_exchange(xs, i, l)
                    else:
                        _cmp_exchange(xs, l, i)
            j //= 2
        k *= 2
    return xs


def _top16_desc(xs):
    runs = [_sort16_desc(xs[g:g + PEER_TOPK]) for g in range(0, len(xs), PEER_TOPK)]
    while len(runs) > 1:
        nxt = []
        for a, b in zip(runs[0::2], runs[1::2]):
            nxt.append(_bitonic_merge_desc([jnp.maximum(a[i], b[PEER_TOPK - 1 - i])
                                            for i in range(PEER_TOPK)]))
        runs = nxt
    return runs[0]


BIG = 3.0e38


def _route_kernel(xt_ref, wqt_ref, keh_ref, c1_ref, p1_ref, s2_ref, p2_ref, s_scr, t_scr):
    qt = jnp.dot(wqt_ref[...], xt_ref[0], preferred_element_type=F32).astype(BF16)
    for a in range(2):
        s = jnp.dot(keh_ref[a], qt[a * EH_ROWS:(a + 1) * EH_ROWS], preferred_element_type=F32)
        for b in range(RT_TM // LANES):
            s_scr[a, b] = s[:, b * LANES:(b + 1) * LANES]

    def block(b, carry):
        s1 = s_scr[0, b].reshape(PEER_N_KEYS, PEER_HEADS, LANES)
        s2 = s_scr[1, b].reshape(PEER_N_KEYS, PEER_HEADS, LANES)
        v1 = _top16_desc([s1[e] for e in range(PEER_N_KEYS)])
        v2 = _top16_desc([s2[e] for e in range(PEER_N_KEYS)])
        cands = [v1[i] + v2[j] for (i, j) in STAIR]
        pad = [jnp.full_like(cands[0], -jnp.inf)] * (-len(cands) % PEER_TOPK)
        best = _top16_desc(cands + pad)
        top, thr = best[0], best[PEER_TOPK - 1]
        z = jnp.zeros_like(top)
        cut = [jnp.full_like(top, BIG) for _ in range(PEER_TOPK)]
        for (i, j), c in zip(STAIR, cands):
            sel = c >= thr
            z = z + jnp.where(sel, jnp.exp(c - top), 0.0)
            cut[i] = jnp.where(sel, jnp.minimum(cut[i], v2[j]), cut[i])
        inv_z = 1.0 / z
        c1 = jnp.full_like(s1, BIG)
        for i in range(PEER_TOPK):
            c1 = jnp.where(s1 == v1[i][None], cut[i][None], c1)
        c1_ref[b] = c1
        p1_ref[b] = jnp.exp(s1 - v1[0][None]) * inv_z[None]
        t_scr[...] = jnp.exp(s2 - v2[0][None]).reshape(EH_ROWS, LANES)
        for h in range(PEER_HEADS):
            s2_ref[b, h] = s_scr[1, b, pl.ds(h, PEER_N_KEYS, stride=PEER_HEADS), :]
            p2_ref[b, h] = t_scr[pl.ds(h, PEER_N_KEYS, stride=PEER_HEADS), :]
        return carry

    lax.fori_loop(0, RT_TM // LANES, block, 0)


def peer_route(xt, wqt, keh):
    nblk = TOKENS // LANES
    nb = RT_TM // LANES
    shp_kh = jax.ShapeDtypeStruct((nblk, PEER_N_KEYS, PEER_HEADS, LANES), F32)
    shp_hk = jax.ShapeDtypeStruct((nblk, PEER_HEADS, PEER_N_KEYS, LANES), F32)
    ospec_kh = pl.BlockSpec((nb, PEER_N_KEYS, PEER_HEADS, LANES), lambda i: (i, 0, 0, 0))
    ospec_hk = pl.BlockSpec((nb, PEER_HEADS, PEER_N_KEYS, LANES), lambda i: (i, 0, 0, 0))
    return pl.pallas_call(
        _route_kernel,
        grid=(TOKENS // RT_TM,),
        in_specs=[
            pl.BlockSpec((1, D_MODEL, RT_TM), lambda i: (i, 0, 0)),
            pl.BlockSpec((2 * EH_ROWS, D_MODEL), lambda i: (0, 0)),
            pl.BlockSpec((2, EH_ROWS, EH_ROWS), lambda i: (0, 0, 0)),
        ],
        out_specs=[ospec_kh, ospec_kh, ospec_hk, ospec_hk],
        out_shape=[shp_kh, shp_kh, shp_hk, shp_hk],
        scratch_shapes=[pltpu.VMEM((2, nb, EH_ROWS, LANES), F32), pltpu.VMEM((EH_ROWS, LANES), F32)],
        compiler_params=_params(("parallel",)),
        name="peer_route",
    )(xt, wqt, keh)


PE_SLAB = LN_TM
PE_NSLAB = 2
PE_TM = PE_NSLAB * PE_SLAB
PE_E1 = 8
PE_TE = PE_E1 * PEER_N_KEYS


def _gelu(x):
    return x * (lax.erf(x * (1.0 / math.sqrt(2.0))) + 1.0) * 0.5


PE_TILES = PEER_N_KEYS // PE_E1
PE_E1G = 2
PE_CHUNK = PE_E1G * PEER_N_KEYS
PE_NCH = PE_TE // PE_CHUNK
PE_DROWS = D_MODEL // PE_NCH
PE_K2 = 64
PE_LAG3 = 2


def _peer_chunk(idx, xt_ref, ut_ref, v_ref, c1_ref, p1_ref, s2_ref, p2_ref, o_ref, h_new, h_old, w_new, w_old):
    sl = idx // PE_NCH
    c = idx % PE_NCH
    c0 = pl.multiple_of(c * PE_CHUNK, PE_CHUNK)
    d0 = pl.multiple_of(c * PE_DROWS, PE_DROWS)
    for tc in range(PE_SLAB // LANES):
        tb = sl * (PE_SLAB // LANES) + tc
        for k0 in range(0, PEER_N_KEYS, PE_K2):
            g = [None] * PE_E1G
            for h in range(PEER_HEADS):
                s2 = s2_ref[tb, h, pl.ds(k0, PE_K2), :]
                p2 = p2_ref[tb, h, pl.ds(k0, PE_K2), :]
                for e in range(PE_E1G):
                    el = c * PE_E1G + e
                    c1 = c1_ref[tb, el, pl.ds(h, 1), :]
                    p1 = p1_ref[tb, el, pl.ds(h, 1), :]
                    t = jnp.where(s2 >= c1, p1 * p2, 0.0)
                    g[e] = t if g[e] is None else g[e] + t
            for e in range(PE_E1G):
                rows = pl.ds(c0 + e * PEER_N_KEYS + k0, PE_K2)
                ts = pl.ds(tc * LANES, LANES)
                w_new[sl, rows, ts] = (g[e] * _gelu(h_old[sl, rows, ts])).astype(BF16)
    h_new[sl, pl.ds(c0, PE_CHUNK), :] = lax.dot_general(ut_ref[c], xt_ref[sl], TN_DIMS,
                                                        preferred_element_type=F32)
    o_ref[sl, pl.ds(d0, PE_DROWS), :] += lax.dot_general(v_ref[c], w_old[sl], TN_DIMS,
                                                         preferred_element_type=F32)


def _peer_step(*refs):
    def body(idx, carry):
        _peer_chunk(idx, *refs)
        return carry

    lax.fori_loop(0, PE_NSLAB * PE_NCH, body, 0)


def _peer_kernel(xt_ref, ut_ref, v_ref, c1_ref, p1_ref, s2_ref, p2_ref, o_ref, h_a, h_b, w_a, w_b):
    s = pl.program_id(0)

    @pl.when(s == 0)
    def _():
        h_b[...] = jnp.zeros_like(h_b)
        w_b[...] = jnp.zeros_like(w_b)

    @pl.when((s == 0) | (s % PE_TILES == PE_LAG3))
    def _():
        o_ref[...] = jnp.zeros_like(o_ref)

    args = (xt_ref, ut_ref, v_ref, c1_ref, p1_ref, s2_ref, p2_ref, o_ref)

    @pl.when(s % 2 == 0)
    def _():
        _peer_step(*args, h_a, h_b, w_a, w_b)

    @pl.when(s % 2 == 1)
    def _():
        _peer_step(*args, h_b, h_a, w_b, w_a)


def peer_experts(xt, ut, v, c1, p1, s2, p2):
    n_items = (TOKENS // PE_TM) * PE_TILES
    nb = PE_TM // LANES

    def item(s, lag):
        w = jnp.clip(s - lag, 0, n_items - 1)
        return w // PE_TILES, w % PE_TILES

    gate_e1 = pl.BlockSpec((nb, PE_E1, PEER_HEADS, LANES), lambda s: (*item(s, 1), 0, 0))
    gate_e2 = pl.BlockSpec((nb, PEER_HEADS, PEER_N_KEYS, LANES), lambda s: (item(s, 1)[0], 0, 0, 0))
    return pl.pallas_call(
        _peer_kernel,
        grid=(n_items + PE_LAG3,),
        in_specs=[
            pl.BlockSpec((PE_NSLAB, D_MODEL, PE_SLAB), lambda s: (item(s, 0)[0], 0, 0)),
            pl.BlockSpec((PE_NCH, D_MODEL, PE_CHUNK), lambda s: (item(s, 0)[1], 0, 0)),
            pl.BlockSpec((PE_NCH, PE_TE, PE_DROWS), lambda s: (item(s, PE_LAG3)[1], 0, 0)),
            gate_e1, gate_e1, gate_e2, gate_e2,
        ],
        out_specs=pl.BlockSpec((PE_NSLAB, D_MODEL, PE_SLAB), lambda s: (item(s, PE_LAG3)[0], 0, 0)),
        out_shape=jax.ShapeDtypeStruct((TOKENS // PE_SLAB, D_MODEL, PE_SLAB), F32),
        scratch_shapes=[pltpu.VMEM((PE_NSLAB, PE_TE, PE_SLAB), F32), pltpu.VMEM((PE_NSLAB, PE_TE, PE_SLAB), F32),
                        pltpu.VMEM((PE_NSLAB, PE_TE, PE_SLAB), BF16), pltpu.VMEM((PE_NSLAB, PE_TE, PE_SLAB), BF16)],
        compiler_params=_params(("arbitrary",)),
        name="peer_experts",
    )(xt, ut, v, c1, p1, s2, p2)


def _res_ln_kernel(x_ref, ft_ref, g_ref, b_ref, y_ref):
    y_ref[...] = _layer_norm(ALPHA * x_ref[...] + ft_ref[0].T, g_ref[...], b_ref[...])


def res_ln(x2d, ft, g, b):
    return pl.pallas_call(
        _res_ln_kernel,
        grid=(TOKENS // LN_TM,),
        in_specs=[
            pl.BlockSpec((LN_TM, D_MODEL), lambda i: (i, 0)),
            pl.BlockSpec((1, D_MODEL, LN_TM), lambda i: (i, 0, 0)),
            pl.BlockSpec((1, D_MODEL), lambda i: (0, 0)),
            pl.BlockSpec((1, D_MODEL), lambda i: (0, 0)),
        ],
        out_specs=pl.BlockSpec((LN_TM, D_MODEL), lambda i: (i, 0)),
        out_shape=jax.ShapeDtypeStruct((TOKENS, D_MODEL), F32),
        compiler_params=_params(("parallel",)),
        name="res_ln",
    )(x2d, ft, g, b)


def _t5_bucket(rel):
    half = T5_BUCKETS // 2
    base = jnp.where(rel > 0, half, 0)
    n = jnp.abs(rel)
    nf = jnp.maximum(n, 1).astype(F32)
    large = T5_MAX_EXACT + (jnp.log(nf / T5_MAX_EXACT) / math.log(T5_MAX_DIST / T5_MAX_EXACT)
                            * (half - T5_MAX_EXACT)).astype(jnp.int32)
    large = jnp.minimum(large, half - 1)
    return base + jnp.where(n < T5_MAX_EXACT, n, large)


def _t5_tiles(t5_bias):
    r = jnp.arange(LANES, dtype=jnp.int32)
    blk = jnp.arange(-2, 3, dtype=jnp.int32)
    rel = blk[:, None, None] * LANES + r[None, None, :] - r[None, :, None]
    onehot = (_t5_bucket(rel)[..., None] == jnp.arange(T5_BUCKETS, dtype=jnp.int32)).astype(F32)
    return jnp.einsum('tqkc,ch->htqk', onehot, t5_bias.astype(F32), precision=lax.Precision.HIGHEST)


def _na_tiles(rpb):
    c = jnp.arange(GRID_W, dtype=jnp.int32)
    col_start = jnp.clip(c - NA_WIN_W // 2, 0, GRID_W - NA_WIN_W)
    in_win = (c[None, :] >= col_start[:, None]) & (c[None, :] < col_start[:, None] + NA_WIN_W)
    dc = jnp.clip(c[None, :] - c[:, None] + (NA_WIN_W - 1), 0, 2 * NA_WIN_W - 2)
    onehot = (dc[..., None] == jnp.arange(2 * NA_WIN_W - 1, dtype=jnp.int32)).astype(F32)
    rows = jnp.stack([rpb[:, NA_WIN_H - 1 - off:2 * NA_WIN_H - 1 - off, :] for off in range(NA_WIN_H)],
                     axis=1).astype(F32)
    bias = jnp.einsum('hoic,qkc->hoqik', rows, onehot, precision=lax.Precision.HIGHEST)
    bias = jnp.where(in_win[None, None, :, None, :], bias, NEG_INF)
    bias = bias.reshape(N_NA_HEADS // 2, 2, NA_WIN_H, GRID_W, NA_KEYS)
    return bias.transpose(0, 2, 1, 3, 4).reshape(N_NA_HEADS // 2, NA_WIN_H, 2 * GRID_W, NA_KEYS)


def kernel(x, w_in, w_out, lam_q1, lam_k1, lam_q2, lam_k2, diff_norm_g, t5_bias, na_rpb,
           ln1_g, ln1_b, peer_wq, peer_keys, peer_u, peer_v, ln2_g, ln2_b):
    xf = x.reshape(TOKENS, D_MODEL)
    t5_tab = _t5_tiles(t5_bias) * LOG2E
    eye = jnp.eye(PEER_HEADS, dtype=F32)
    for l in range(DEPTH):
        lam_init = 0.8 - 0.6 * math.exp(-0.3 * l)
        lam = (jnp.exp(jnp.sum(lam_q1[l] * lam_k1[l])) - jnp.exp(jnp.sum(lam_q2[l] * lam_k2[l]))
               + lam_init).reshape(1).astype(F32)
        proj = in_proj(xf, w_in[l].astype(BF16)).reshape(BATCH, SEQ, IN_PROJ_W)
        g2 = jnp.tile(diff_norm_g[l], 2).reshape(1, LANES)
        o_diff = diff_attention(proj, lam, t5_tab, g2, lam_init)
        o_na = na_attention(proj, _na_tiles(na_rpb[l]))
        x1, x1t = out_proj_ln(o_diff.reshape(TOKENS, -1), o_na.reshape(TOKENS, -1), xf,
                              w_out[l].astype(BF16), ln1_g[l].reshape(1, -1), ln1_b[l].reshape(1, -1))
        wqt = (peer_wq[l].reshape(D_MODEL, PEER_HEADS, 2, PEER_N_KEYS)
               .transpose(2, 1, 3, 0).reshape(2 * EH_ROWS, D_MODEL).astype(BF16))
        keh = jnp.einsum('aed,hg->aehgd', peer_keys[l], eye).reshape(2, EH_ROWS, EH_ROWS).astype(BF16)
        c1, p1, s2, p2 = peer_route(x1t, wqt, keh)
        ut = (peer_u[l].reshape(PEER_N_EXPERTS // PE_CHUNK, PE_CHUNK, D_MODEL)
              .transpose(0, 2, 1).astype(BF16))
        v = (peer_v[l].reshape(PE_TILES, PE_TE, PE_NCH, PE_DROWS)
             .transpose(0, 2, 1, 3).astype(BF16).reshape(PE_TILES * PE_NCH, PE_TE, PE_DROWS))
        ft = peer_experts(x1t, ut, v, c1, p1, s2, p2)
        xf = res_ln(x1, ft, ln2_g[l].reshape(1, -1), ln2_b[l].reshape(1, -1))
    return xf.reshape(BATCH, SEQ, D_MODEL)
```

```python
import functools
import math

import jax
import jax.numpy as jnp
from jax import lax
from jax.experimental import pallas as pl
from jax.experimental.pallas import tpu as pltpu

F32 = jnp.float32
BF16 = jnp.bfloat16

D_MODEL = 1024
BATCH = 8
SEQ = 2048
TOKENS = BATCH * SEQ
DEPTH = 2
HEAD_DIM = 64
N_DIFF_HEADS = 8
DIFF_QK_DIM = 32
N_NA_HEADS = 8
IN_PROJ_W = 3072
GRID_W = 64
GRID_ROWS = SEQ // GRID_W
NA_WIN_H = 8
NA_WIN_W = 16
NA_KEYS = NA_WIN_H * GRID_W
T5_BUCKETS = 32
T5_MAX_EXACT = 8
T5_MAX_DIST = 128
PEER_HEADS = 8
PEER_N_KEYS = 128
PEER_N_EXPERTS = PEER_N_KEYS * PEER_N_KEYS
PEER_TOPK = 16
LN_EPS = 1e-5
NEG_INF = -1e30
ALPHA = (2 * DEPTH) ** 0.25
LOG2E = 1.4426950408889634

LANES = 128
VMEM_LIMIT = 56 * 1024 * 1024

DQ_BLK, DK_BLK, DV_BLK, NQ_BLK, NK_BLK, NV_BLK = 0, 4, 8, 12, 16, 20

STAIR = [(i, j) for i in range(PEER_TOPK) for j in range(PEER_TOPK)
         if (i + 1) * (j + 1) <= PEER_TOPK]

NT_DIMS = (((1,), (1,)), ((), ()))
TN_DIMS = (((0,), (0,)), ((), ()))


def _params(sem, vmem=VMEM_LIMIT, flags=None):
    return pltpu.CompilerParams(dimension_semantics=sem, vmem_limit_bytes=vmem, flags=flags)


IN_TM = 512


def _in_proj_kernel(x_ref, w_ref, o_ref):
    o_ref[...] = jnp.dot(x_ref[...].astype(BF16), w_ref[...],
                         preferred_element_type=F32).astype(BF16)


def in_proj(x2d, w):
    return pl.pallas_call(
        _in_proj_kernel,
        grid=(TOKENS // IN_TM,),
        in_specs=[pl.BlockSpec((IN_TM, D_MODEL), lambda i: (i, 0)),
                  pl.BlockSpec((D_MODEL, IN_PROJ_W), lambda i: (0, 0))],
        out_specs=pl.BlockSpec((IN_TM, IN_PROJ_W), lambda i: (i, 0)),
        out_shape=jax.ShapeDtypeStruct((TOKENS, IN_PROJ_W), BF16),
        compiler_params=_params(("parallel",)),
        name="in_proj",
    )(x2d, w)


DIFF_TQ = 512
KEY_BLOCKS = SEQ // LANES


def _diff_kernel(lam_ref, q_ref, k_ref, v_ref, tab_ref, g_ref, o_ref, *, lam_init):
    qi = pl.program_id(2)
    lam = lam_ref[0]
    q = q_ref[0].astype(F32) * (DIFF_QK_DIM ** -0.5 * LOG2E)
    k = k_ref[0]
    v = v_ref[0]
    lane = lax.broadcasted_iota(jnp.int32, (1, LANES), 1)

    def scores(hh):
        rows = []
        for sb in range(DIFF_TQ // LANES):
            qblk = qi * (DIFF_TQ // LANES) + sb
            tiles = [tab_ref[hh, jnp.clip(kj - qblk, -2, 2) + 2] for kj in range(KEY_BLOCKS)]
            rows.append(jnp.concatenate(tiles, axis=1))
        bias = jnp.concatenate(rows, axis=0)
        out = []
        for m in range(2):
            c0 = hh * HEAD_DIM + m * DIFF_QK_DIM
            msk = (lane >= c0) & (lane < c0 + DIFF_QK_DIM)
            qs = jnp.where(msk, q, 0.0).astype(BF16)
            out.append(lax.dot_general(qs, k, NT_DIMS, preferred_element_type=F32) + bias)
        return out

    def weights(ss):
        es = [jnp.exp2(s - jnp.max(s, axis=-1, keepdims=True)) for s in ss]
        zs = [jnp.sum(e, axis=-1, keepdims=True) for e in es]
        return (es[0] - (lam * zs[0] / zs[1]) * es[1]).astype(BF16), 1.0 / zs[0]

    def head_out(hh, w, rz):
        o = jnp.dot(w, v, preferred_element_type=F32) * rz
        hm = (lane >= hh * HEAD_DIM) & (lane < (hh + 1) * HEAD_DIM)
        ms = jnp.sum(jnp.where(hm, o * o, 0.0), axis=-1, keepdims=True) * (1.0 / HEAD_DIM)
        return jnp.where(hm, o * lax.rsqrt(ms + LN_EPS), 0.0)

    s0 = scores(0)
    s1 = scores(1)
    w0, rz0 = weights(s0)
    o0 = head_out(0, w0, rz0)
    w1, rz1 = weights(s1)
    o1 = head_out(1, w1, rz1)
    o_ref[0] = ((o0 + o1) * g_ref[...] * (1.0 - lam_init)).astype(BF16)


def diff_attention(proj, lam, t5_tab, norm_g2, lam_init):
    nq = SEQ // DIFF_TQ
    return pl.pallas_call(
        functools.partial(_diff_kernel, lam_init=lam_init),
        grid=(BATCH, N_DIFF_HEADS // 2, nq),
        in_specs=[
            pl.BlockSpec(memory_space=pltpu.SMEM),
            pl.BlockSpec((1, DIFF_TQ, LANES), lambda b, h, i: (b, i, DQ_BLK + h)),
            pl.BlockSpec((1, SEQ, LANES), lambda b, h, i: (b, 0, DK_BLK + h)),
            pl.BlockSpec((1, SEQ, LANES), lambda b, h, i: (b, 0, DV_BLK + h)),
            pl.BlockSpec((2, 5, LANES, LANES), lambda b, h, i: (h, 0, 0, 0)),
            pl.BlockSpec((1, LANES), lambda b, h, i: (0, 0)),
        ],
        out_specs=pl.BlockSpec((1, DIFF_TQ, LANES), lambda b, h, i: (b, i, h)),
        out_shape=jax.ShapeDtypeStruct((BATCH, SEQ, N_DIFF_HEADS * HEAD_DIM), BF16),
        compiler_params=_params(("parallel", "parallel", "arbitrary")),
        name="diff_attention",
    )(lam, proj, proj, proj, t5_tab, norm_g2)


NA_AHEAD = 4

def _na_kernel(q_ref, k_ref, v_ref, tab_ref, o_ref):
    lane = lax.broadcasted_iota(jnp.int32, (1, LANES), 1)
    h0 = lane < HEAD_DIM

    def win(r):
        return min(max(r - NA_WIN_H // 2, 0), GRID_ROWS - NA_WIN_H)

    def scores(r):
        rs = win(r)
        qr = q_ref[0, r * GRID_W:(r + 1) * GRID_W, :].astype(F32) * (HEAD_DIM ** -0.5)
        kw = k_ref[0, rs * GRID_W:rs * GRID_W + NA_KEYS, :]
        q2 = jnp.concatenate([jnp.where(h0, qr, 0.0), jnp.where(h0, 0.0, qr)], axis=0).astype(BF16)
        return lax.dot_general(q2, kw, NT_DIMS, preferred_element_type=F32) + tab_ref[0, r - rs]

    pending = [scores(r) for r in range(NA_AHEAD)]
    for r in range(GRID_ROWS):
        s = pending.pop(0)
        if r + NA_AHEAD < GRID_ROWS:
            pending.append(scores(r + NA_AHEAD))
        rs = win(r)
        vw = v_ref[0, rs * GRID_W:rs * GRID_W + NA_KEYS, :]
        e = jnp.exp(s - jnp.max(s, axis=-1, keepdims=True))
        p = e * (1.0 / jnp.sum(e, axis=-1, keepdims=True))
        o = jnp.dot(p.astype(BF16), vw, preferred_element_type=F32)
        o_ref[0, r * GRID_W:(r + 1) * GRID_W, :] = jnp.where(h0, o[:GRID_W], o[GRID_W:]).astype(BF16)


def na_attention(proj, na_tab):
    return pl.pallas_call(
        _na_kernel,
        grid=(BATCH, N_NA_HEADS // 2),
        in_specs=[
            pl.BlockSpec((1, SEQ, LANES), lambda b, h: (b, 0, NQ_BLK + h)),
            pl.BlockSpec((1, SEQ, LANES), lambda b, h: (b, 0, NK_BLK + h)),
            pl.BlockSpec((1, SEQ, LANES), lambda b, h: (b, 0, NV_BLK + h)),
            pl.BlockSpec((1, NA_WIN_H, 2 * GRID_W, NA_KEYS), lambda b, h: (h, 0, 0, 0)),
        ],
        out_specs=pl.BlockSpec((1, SEQ, LANES), lambda b, h: (b, 0, h)),
        out_shape=jax.ShapeDtypeStruct((BATCH, SEQ, N_NA_HEADS * HEAD_DIM), BF16),
        compiler_params=_params(("parallel", "parallel")),
        name="na_attention",
    )(proj, proj, proj, na_tab)


LN_TM = 512


def _layer_norm(y, g, b):
    mu = jnp.mean(y, axis=-1, keepdims=True)
    yc = y - mu
    var = jnp.mean(yc * yc, axis=-1, keepdims=True)
    return yc * lax.rsqrt(var + LN_EPS) * g + b


def _out_ln_kernel(od_ref, on_ref, x_ref, w_ref, g_ref, b_ref, y_ref, yt_ref):
    half = N_DIFF_HEADS * HEAD_DIM
    h = jnp.dot(od_ref[...], w_ref[0:half, :], preferred_element_type=F32)
    h = h + jnp.dot(on_ref[...], w_ref[half:2 * half, :], preferred_element_type=F32)
    y = _layer_norm(ALPHA * x_ref[...] + h, g_ref[...], b_ref[...])
    y_ref[...] = y
    yt_ref[0] = y.T.astype(BF16)


def out_proj_ln(o_diff, o_na, x2d, w, g, b):
    half = N_DIFF_HEADS * HEAD_DIM
    return pl.pallas_call(
        _out_ln_kernel,
        grid=(TOKENS // LN_TM,),
        in_specs=[
            pl.BlockSpec((LN_TM, half), lambda i: (i, 0)),
            pl.BlockSpec((LN_TM, half), lambda i: (i, 0)),
            pl.BlockSpec((LN_TM, D_MODEL), lambda i: (i, 0)),
            pl.BlockSpec((D_MODEL, D_MODEL), lambda i: (0, 0)),
            pl.BlockSpec((1, D_MODEL), lambda i: (0, 0)),
            pl.BlockSpec((1, D_MODEL), lambda i: (0, 0)),
        ],
        out_specs=[pl.BlockSpec((LN_TM, D_MODEL), lambda i: (i, 0)),
                   pl.BlockSpec((1, D_MODEL, LN_TM), lambda i: (i, 0, 0))],
        out_shape=[jax.ShapeDtypeStruct((TOKENS, D_MODEL), F32),
                   jax.ShapeDtypeStruct((TOKENS // LN_TM, D_MODEL, LN_TM), BF16)],
        compiler_params=_params(("parallel",)),
        name="out_proj_ln",
    )(o_diff, o_na, x2d, w, g, b)


RT_TM = LN_TM
EH_ROWS = PEER_N_KEYS * PEER_HEADS


def _cmp_exchange(xs, i, l):
    a, b = xs[i], xs[l]
    xs[i], xs[l] = jnp.maximum(a, b), jnp.minimum(a, b)


def _bitonic_merge_desc(xs):
    xs = list(xs)
    j = len(xs) // 2
    while j >= 1:
        for i in range(len(xs)):
            if i & j == 0:
                _cmp_exchange(xs, i, i | j)
        j //= 2
    return xs


def _sort16_desc(xs):
    xs = list(xs)
    k = 2
    while k <= PEER_TOPK:
        j = k // 2
        while j >= 1:
            for i in range(PEER_TOPK):
                l = i ^ j
                if l > i:
                    if i & k == 0 or k == PEER_TOPK:
                        _cmp_exchange(xs, i, l)
                    else:
                        _cmp_exchange(xs, l, i)
            j //= 2
        k *= 2
    return xs


def _top16_desc(xs):
    runs = [_sort16_desc(xs[g:g + PEER_TOPK]) for g in range(0, len(xs), PEER_TOPK)]
    while len(runs) > 1:
        nxt = []
        for a, b in zip(runs[0::2], runs[1::2]):
            nxt.append(_bitonic_merge_desc([jnp.maximum(a[i], b[PEER_TOPK - 1 - i])
                                            for i in range(PEER_TOPK)]))
        runs = nxt
    return runs[0]


BIG = 3.0e38


def _route_kernel(xt_ref, wqt_ref, keh_ref, c1_ref, p1_ref, s2_ref, p2_ref, s_scr, t_scr):
    qt = jnp.dot(wqt_ref[...], xt_ref[0], preferred_element_type=F32).astype(BF16)
    for a in range(2):
        s = jnp.dot(keh_ref[a], qt[a * EH_ROWS:(a + 1) * EH_ROWS], preferred_element_type=F32)
        for b in range(RT_TM // LANES):
            s_scr[a, b] = s[:, b * LANES:(b + 1) * LANES]

    def block(b, carry):
        s1 = s_scr[0, b].reshape(PEER_N_KEYS, PEER_HEADS, LANES)
        s2 = s_scr[1, b].reshape(PEER_N_KEYS, PEER_HEADS, LANES)
        v1 = _top16_desc([s1[e] for e in range(PEER_N_KEYS)])
        v2 = _top16_desc([s2[e] for e in range(PEER_N_KEYS)])
        cands = [v1[i] + v2[j] for (i, j) in STAIR]
        pad = [jnp.full_like(cands[0], -jnp.inf)] * (-len(cands) % PEER_TOPK)
        best = _top16_desc(cands + pad)
        top, thr = best[0], best[PEER_TOPK - 1]
        z = jnp.zeros_like(top)
        cut = [jnp.full_like(top, BIG) for _ in range(PEER_TOPK)]
        for (i, j), c in zip(STAIR, cands):
            sel = c >= thr
            z = z + jnp.where(sel, jnp.exp(c - top), 0.0)
            cut[i] = jnp.where(sel, jnp.minimum(cut[i], v2[j]), cut[i])
        inv_z = 1.0 / z
        c1 = jnp.full_like(s1, BIG)
        for i in range(PEER_TOPK):
            c1 = jnp.where(s1 == v1[i][None], cut[i][None], c1)
        c1_ref[b] = c1
        p1_ref[b] = jnp.exp(s1 - v1[0][None]) * inv_z[None]
        t_scr[...] = jnp.exp(s2 - v2[0][None]).reshape(EH_ROWS, LANES)
        for h in range(PEER_HEADS):
            s2_ref[b, h] = s_scr[1, b, pl.ds(h, PEER_N_KEYS, stride=PEER_HEADS), :]
            p2_ref[b, h] = t_scr[pl.ds(h, PEER_N_KEYS, stride=PEER_HEADS), :]
        return carry

    lax.fori_loop(0, RT_TM // LANES, block, 0)


def peer_route(xt, wqt, keh):
    nblk = TOKENS // LANES
    nb = RT_TM // LANES
    shp_kh = jax.ShapeDtypeStruct((nblk, PEER_N_KEYS, PEER_HEADS, LANES), F32)
    shp_hk = jax.ShapeDtypeStruct((nblk, PEER_HEADS, PEER_N_KEYS, LANES), F32)
    ospec_kh = pl.BlockSpec((nb, PEER_N_KEYS, PEER_HEADS, LANES), lambda i: (i, 0, 0, 0))
    ospec_hk = pl.BlockSpec((nb, PEER_HEADS, PEER_N_KEYS, LANES), lambda i: (i, 0, 0, 0))
    return pl.pallas_call(
        _route_kernel,
        grid=(TOKENS // RT_TM,),
        in_specs=[
            pl.BlockSpec((1, D_MODEL, RT_TM), lambda i: (i, 0, 0)),
            pl.BlockSpec((2 * EH_ROWS, D_MODEL), lambda i: (0, 0)),
            pl.BlockSpec((2, EH_ROWS, EH_ROWS), lambda i: (0, 0, 0)),
        ],
        out_specs=[ospec_kh, ospec_kh, ospec_hk, ospec_hk],
        out_shape=[shp_kh, shp_kh, shp_hk, shp_hk],
        scratch_shapes=[pltpu.VMEM((2, nb, EH_ROWS, LANES), F32), pltpu.VMEM((EH_ROWS, LANES), F32)],
        compiler_params=_params(("parallel",)),
        name="peer_route",
    )(xt, wqt, keh)


PE_SLAB = LN_TM
PE_NSLAB = 2
PE_TM = PE_NSLAB * PE_SLAB
PE_E1 = 8
PE_TE = PE_E1 * PEER_N_KEYS


def _gelu(x):
    return x * (lax.erf(x * (1.0 / math.sqrt(2.0))) + 1.0) * 0.5


PE_TILES = PEER_N_KEYS // PE_E1
PE_E1G = 2
PE_CHUNK = 4 * PEER_N_KEYS
PE_NCH = PE_TE // PE_CHUNK
PE_DROWS = D_MODEL // PE_NCH
PE_K2 = 64
PE_LAG3 = 2


def _peer_chunk(idx, xt_ref, ut_ref, v_ref, c1_ref, p1_ref, s2_ref, p2_ref, o_ref, h_new, h_old, w_new, w_old):
    sl = idx // PE_NCH
    c = idx % PE_NCH
    c0 = pl.multiple_of(c * PE_CHUNK, PE_CHUNK)
    d0 = pl.multiple_of(c * PE_DROWS, PE_DROWS)
    for tc in range(PE_SLAB // LANES):
        tb = sl * (PE_SLAB // LANES) + tc
        ts = pl.ds(tc * LANES, LANES)
        for e0 in range(0, PE_CHUNK // PEER_N_KEYS, PE_E1G):
            for k0 in range(0, PEER_N_KEYS, PE_K2):
                g = [None] * PE_E1G
                for h in range(PEER_HEADS):
                    s2 = s2_ref[tb, h, pl.ds(k0, PE_K2), :]
                    p2 = p2_ref[tb, h, pl.ds(k0, PE_K2), :]
                    for e in range(PE_E1G):
                        el = c * (PE_CHUNK // PEER_N_KEYS) + e0 + e
                        c1 = c1_ref[tb, el, pl.ds(h, 1), :]
                        p1 = p1_ref[tb, el, pl.ds(h, 1), :]
                        t = jnp.where(s2 >= c1, p1 * p2, 0.0)
                        g[e] = t if g[e] is None else g[e] + t
                for e in range(PE_E1G):
                    rows = pl.ds(c0 + (e0 + e) * PEER_N_KEYS + k0, PE_K2)
                    w_new[sl, rows, ts] = (g[e] * _gelu(h_old[sl, rows, ts])).astype(BF16)
    h_new[sl, pl.ds(c0, PE_CHUNK), :] = lax.dot_general(ut_ref[c], xt_ref[sl], TN_DIMS,
                                                        preferred_element_type=F32)
    o_ref[sl, pl.ds(d0, PE_DROWS), :] += lax.dot_general(v_ref[c], w_old[sl], TN_DIMS,
                                                         preferred_element_type=F32)


def _peer_step(*refs):
    def body(idx, carry):
        _peer_chunk(idx, *refs)
        return carry

    lax.fori_loop(0, PE_NSLAB * PE_NCH, body, 0)


def _peer_kernel(xt_ref, ut_ref, v_ref, c1_ref, p1_ref, s2_ref, p2_ref, o_ref, h_a, h_b, w_a, w_b):
    s = pl.program_id(0)

    @pl.when(s == 0)
    def _():
        h_b[...] = jnp.zeros_like(h_b)
        w_b[...] = jnp.zeros_like(w_b)

    @pl.when((s == 0) | (s % PE_TILES == PE_LAG3))
    def _():
        o_ref[...] = jnp.zeros_like(o_ref)

    args = (xt_ref, ut_ref, v_ref, c1_ref, p1_ref, s2_ref, p2_ref, o_ref)

    @pl.when(s % 2 == 0)
    def _():
        _peer_step(*args, h_a, h_b, w_a, w_b)

    @pl.when(s % 2 == 1)
    def _():
        _peer_step(*args, h_b, h_a, w_b, w_a)


def peer_experts(xt, ut, v, c1, p1, s2, p2):
    n_items = (TOKENS // PE_TM) * PE_TILES
    nb = PE_TM // LANES

    def item(s, lag):
        w = jnp.clip(s - lag, 0, n_items - 1)
        return w // PE_TILES, w % PE_TILES

    gate_e1 = pl.BlockSpec((nb, PE_E1, PEER_HEADS, LANES), lambda s: (*item(s, 1), 0, 0))
    gate_e2 = pl.BlockSpec((nb, PEER_HEADS, PEER_N_KEYS, LANES), lambda s: (item(s, 1)[0], 0, 0, 0))
    return pl.pallas_call(
        _peer_kernel,
        grid=(n_items + PE_LAG3,),
        in_specs=[
            pl.BlockSpec((PE_NSLAB, D_MODEL, PE_SLAB), lambda s: (item(s, 0)[0], 0, 0)),
            pl.BlockSpec((PE_NCH, D_MODEL, PE_CHUNK), lambda s: (item(s, 0)[1], 0, 0)),
            pl.BlockSpec((PE_NCH, PE_TE, PE_DROWS), lambda s: (item(s, PE_LAG3)[1], 0, 0)),
            gate_e1, gate_e1, gate_e2, gate_e2,
        ],
        out_specs=pl.BlockSpec((PE_NSLAB, D_MODEL, PE_SLAB), lambda s: (item(s, PE_LAG3)[0], 0, 0)),
        out_shape=jax.ShapeDtypeStruct((TOKENS // PE_SLAB, D_MODEL, PE_SLAB), F32),
        scratch_shapes=[pltpu.VMEM((PE_NSLAB, PE_TE, PE_SLAB), F32), pltpu.VMEM((PE_NSLAB, PE_TE, PE_SLAB), F32),
                        pltpu.VMEM((PE_NSLAB, PE_TE, PE_SLAB), BF16), pltpu.VMEM((PE_NSLAB, PE_TE, PE_SLAB), BF16)],
        compiler_params=_params(("arbitrary",)),
        name="peer_experts",
    )(xt, ut, v, c1, p1, s2, p2)


def _res_ln_kernel(x_ref, ft_ref, g_ref, b_ref, y_ref):
    y_ref[...] = _layer_norm(ALPHA * x_ref[...] + ft_ref[0].T, g_ref[...], b_ref[...])


def res_ln(x2d, ft, g, b):
    return pl.pallas_call(
        _res_ln_kernel,
        grid=(TOKENS // LN_TM,),
        in_specs=[
            pl.BlockSpec((LN_TM, D_MODEL), lambda i: (i, 0)),
            pl.BlockSpec((1, D_MODEL, LN_TM), lambda i: (i, 0, 0)),
            pl.BlockSpec((1, D_MODEL), lambda i: (0, 0)),
            pl.BlockSpec((1, D_MODEL), lambda i: (0, 0)),
        ],
        out_specs=pl.BlockSpec((LN_TM, D_MODEL), lambda i: (i, 0)),
        out_shape=jax.ShapeDtypeStruct((TOKENS, D_MODEL), F32),
        compiler_params=_params(("parallel",)),
        name="res_ln",
    )(x2d, ft, g, b)


def _t5_bucket(rel):
    half = T5_BUCKETS // 2
    base = jnp.where(rel > 0, half, 0)
    n = jnp.abs(rel)
    nf = jnp.maximum(n, 1).astype(F32)
    large = T5_MAX_EXACT + (jnp.log(nf / T5_MAX_EXACT) / math.log(T5_MAX_DIST / T5_MAX_EXACT)
                            * (half - T5_MAX_EXACT)).astype(jnp.int32)
    large = jnp.minimum(large, half - 1)
    return base + jnp.where(n < T5_MAX_EXACT, n, large)


def _t5_tiles(t5_bias):
    r = jnp.arange(LANES, dtype=jnp.int32)
    blk = jnp.arange(-2, 3, dtype=jnp.int32)
    rel = blk[:, None, None] * LANES + r[None, None, :] - r[None, :, None]
    onehot = (_t5_bucket(rel)[..., None] == jnp.arange(T5_BUCKETS, dtype=jnp.int32)).astype(F32)
    return jnp.einsum('tqkc,ch->htqk', onehot, t5_bias.astype(F32), precision=lax.Precision.HIGHEST)


def _na_tiles(rpb):
    c = jnp.arange(GRID_W, dtype=jnp.int32)
    col_start = jnp.clip(c - NA_WIN_W // 2, 0, GRID_W - NA_WIN_W)
    in_win = (c[None, :] >= col_start[:, None]) & (c[None, :] < col_start[:, None] + NA_WIN_W)
    dc = jnp.clip(c[None, :] - c[:, None] + (NA_WIN_W - 1), 0, 2 * NA_WIN_W - 2)
    onehot = (dc[..., None] == jnp.arange(2 * NA_WIN_W - 1, dtype=jnp.int32)).astype(F32)
    rows = jnp.stack([rpb[:, NA_WIN_H - 1 - off:2 * NA_WIN_H - 1 - off, :] for off in range(NA_WIN_H)],
                     axis=1).astype(F32)
    bias = jnp.einsum('hoic,qkc->hoqik', rows, onehot, precision=lax.Precision.HIGHEST)
    bias = jnp.where(in_win[None, None, :, None, :], bias, NEG_INF)
    bias = bias.reshape(N_NA_HEADS // 2, 2, NA_WIN_H, GRID_W, NA_KEYS)
    return bias.transpose(0, 2, 1, 3, 4).reshape(N_NA_HEADS // 2, NA_WIN_H, 2 * GRID_W, NA_KEYS)


def kernel(x, w_in, w_out, lam_q1, lam_k1, lam_q2, lam_k2, diff_norm_g, t5_bias, na_rpb,
           ln1_g, ln1_b, peer_wq, peer_keys, peer_u, peer_v, ln2_g, ln2_b):
    xf = x.reshape(TOKENS, D_MODEL)
    t5_tab = _t5_tiles(t5_bias) * LOG2E
    eye = jnp.eye(PEER_HEADS, dtype=F32)
    for l in range(DEPTH):
        lam_init = 0.8 - 0.6 * math.exp(-0.3 * l)
        lam = (jnp.exp(jnp.sum(lam_q1[l] * lam_k1[l])) - jnp.exp(jnp.sum(lam_q2[l] * lam_k2[l]))
               + lam_init).reshape(1).astype(F32)
        proj = in_proj(xf, w_in[l].astype(BF16)).reshape(BATCH, SEQ, IN_PROJ_W)
        g2 = jnp.tile(diff_norm_g[l], 2).reshape(1, LANES)
        o_diff = diff_attention(proj, lam, t5_tab, g2, lam_init)
        o_na = na_attention(proj, _na_tiles(na_rpb[l]))
        x1, x1t = out_proj_ln(o_diff.reshape(TOKENS, -1), o_na.reshape(TOKENS, -1), xf,
                              w_out[l].astype(BF16), ln1_g[l].reshape(1, -1), ln1_b[l].reshape(1, -1))
        wqt = (peer_wq[l].reshape(D_MODEL, PEER_HEADS, 2, PEER_N_KEYS)
               .transpose(2, 1, 3, 0).reshape(2 * EH_ROWS, D_MODEL).astype(BF16))
        keh = jnp.einsum('aed,hg->aehgd', peer_keys[l], eye).reshape(2, EH_ROWS, EH_ROWS).astype(BF16)
        c1, p1, s2, p2 = peer_route(x1t, wqt, keh)
        ut = (peer_u[l].reshape(PEER_N_EXPERTS // PE_CHUNK, PE_CHUNK, D_MODEL)
              .transpose(0, 2, 1).astype(BF16))
        v = (peer_v[l].reshape(PE_TILES, PE_TE, PE_NCH, PE_DROWS)
             .transpose(0, 2, 1, 3).astype(BF16).reshape(PE_TILES * PE_NCH, PE_TE, PE_DROWS))
        ft = peer_experts(x1t, ut, v, c1, p1, s2, p2)
        xf = res_ln(x1, ft, ln2_g[l].reshape(1, -1), ln2_b[l].reshape(1, -1))
    return xf.reshape(BATCH, SEQ, D_MODEL)
```

```python
import functools
import math

import jax
import jax.numpy as jnp
from jax import lax
from jax.experimental import pallas as pl
from jax.experimental.pallas import tpu as pltpu

F32 = jnp.float32
BF16 = jnp.bfloat16

D_MODEL = 1024
BATCH = 8
SEQ = 2048
TOKENS = BATCH * SEQ
DEPTH = 2
HEAD_DIM = 64
N_DIFF_HEADS = 8
DIFF_QK_DIM = 32
N_NA_HEADS = 8
IN_PROJ_W = 3072
GRID_W = 64
GRID_ROWS = SEQ // GRID_W
NA_WIN_H = 8
NA_WIN_W = 16
NA_KEYS = NA_WIN_H * GRID_W
T5_BUCKETS = 32
T5_MAX_EXACT = 8
T5_MAX_DIST = 128
PEER_HEADS = 8
PEER_N_KEYS = 128
PEER_N_EXPERTS = PEER_N_KEYS * PEER_N_KEYS
PEER_TOPK = 16
LN_EPS = 1e-5
NEG_INF = -1e30
ALPHA = (2 * DEPTH) ** 0.25
LOG2E = 1.4426950408889634

LANES = 128
VMEM_LIMIT = 56 * 1024 * 1024

DQ_BLK, DK_BLK, DV_BLK, NQ_BLK, NK_BLK, NV_BLK = 0, 4, 8, 12, 16, 20

STAIR = [(i, j) for i in range(PEER_TOPK) for j in range(PEER_TOPK)
         if (i + 1) * (j + 1) <= PEER_TOPK]

NT_DIMS = (((1,), (1,)), ((), ()))
TN_DIMS = (((0,), (0,)), ((), ()))


def _params(sem):
    return pltpu.CompilerParams(dimension_semantics=sem, vmem_limit_bytes=VMEM_LIMIT)


IN_TM = 512


def _in_proj_kernel(x_ref, w_ref, o_ref):
    o_ref[...] = jnp.dot(x_ref[...].astype(BF16), w_ref[...],
                         preferred_element_type=F32).astype(BF16)


def in_proj(x2d, w):
    return pl.pallas_call(
        _in_proj_kernel,
        grid=(TOKENS // IN_TM,),
        in_specs=[pl.BlockSpec((IN_TM, D_MODEL), lambda i: (i, 0)),
                  pl.BlockSpec((D_MODEL, IN_PROJ_W), lambda i: (0, 0))],
        out_specs=pl.BlockSpec((IN_TM, IN_PROJ_W), lambda i: (i, 0)),
        out_shape=jax.ShapeDtypeStruct((TOKENS, IN_PROJ_W), BF16),
        compiler_params=_params(("parallel",)),
        name="in_proj",
    )(x2d, w)


DIFF_TQ = 512
KEY_BLOCKS = SEQ // LANES


def _diff_kernel(lam_ref, q_ref, k_ref, v_ref, tab_ref, g_ref, o_ref, *, lam_init):
    qi = pl.program_id(2)
    lam = lam_ref[0]
    q = q_ref[0].astype(F32) * (DIFF_QK_DIM ** -0.5 * LOG2E)
    k = k_ref[0]
    v = v_ref[0]
    lane = lax.broadcasted_iota(jnp.int32, (1, LANES), 1)

    def scores(hh):
        rows = []
        for sb in range(DIFF_TQ // LANES):
            qblk = qi * (DIFF_TQ // LANES) + sb
            tiles = [tab_ref[hh, jnp.clip(kj - qblk, -2, 2) + 2] for kj in range(KEY_BLOCKS)]
            rows.append(jnp.concatenate(tiles, axis=1))
        bias = jnp.concatenate(rows, axis=0)
        out = []
        for m in range(2):
            c0 = hh * HEAD_DIM + m * DIFF_QK_DIM
            msk = (lane >= c0) & (lane < c0 + DIFF_QK_DIM)
            qs = jnp.where(msk, q, 0.0).astype(BF16)
            out.append(lax.dot_general(qs, k, NT_DIMS, preferred_element_type=F32) + bias)
        return out

    def weights(ss):
        es = [jnp.exp2(s - jnp.max(s, axis=-1, keepdims=True)) for s in ss]
        zs = [jnp.sum(e, axis=-1, keepdims=True) for e in es]
        return (es[0] - (lam * zs[0] / zs[1]) * es[1]).astype(BF16), 1.0 / zs[0]

    def head_out(hh, w, rz):
        o = jnp.dot(w, v, preferred_element_type=F32) * rz
        hm = (lane >= hh * HEAD_DIM) & (lane < (hh + 1) * HEAD_DIM)
        ms = jnp.sum(jnp.where(hm, o * o, 0.0), axis=-1, keepdims=True) * (1.0 / HEAD_DIM)
        return jnp.where(hm, o * lax.rsqrt(ms + LN_EPS), 0.0)

    s0 = scores(0)
    s1 = scores(1)
    w0, rz0 = weights(s0)
    o0 = head_out(0, w0, rz0)
    w1, rz1 = weights(s1)
    o1 = head_out(1, w1, rz1)
    o_ref[0] = ((o0 + o1) * g_ref[...] * (1.0 - lam_init)).astype(BF16)


def diff_attention(proj, lam, t5_tab, norm_g2, lam_init):
    nq = SEQ // DIFF_TQ
    return pl.pallas_call(
        functools.partial(_diff_kernel, lam_init=lam_init),
        grid=(BATCH, N_DIFF_HEADS // 2, nq),
        in_specs=[
            pl.BlockSpec(memory_space=pltpu.SMEM),
            pl.BlockSpec((1, DIFF_TQ, LANES), lambda b, h, i: (b, i, DQ_BLK + h)),
            pl.BlockSpec((1, SEQ, LANES), lambda b, h, i: (b, 0, DK_BLK + h)),
            pl.BlockSpec((1, SEQ, LANES), lambda b, h, i: (b, 0, DV_BLK + h)),
            pl.BlockSpec((2, 5, LANES, LANES), lambda b, h, i: (h, 0, 0, 0)),
            pl.BlockSpec((1, LANES), lambda b, h, i: (0, 0)),
        ],
        out_specs=pl.BlockSpec((1, DIFF_TQ, LANES), lambda b, h, i: (b, i, h)),
        out_shape=jax.ShapeDtypeStruct((BATCH, SEQ, N_DIFF_HEADS * HEAD_DIM), BF16),
        compiler_params=_params(("parallel", "parallel", "arbitrary")),
        name="diff_attention",
    )(lam, proj, proj, proj, t5_tab, norm_g2)


NA_AHEAD = 4


def _na_kernel(q_ref, k_ref, v_ref, tab_ref, o_ref):
    lane = lax.broadcasted_iota(jnp.int32, (1, LANES), 1)
    h0 = lane < HEAD_DIM

    def win(r):
        return min(max(r - NA_WIN_H // 2, 0), GRID_ROWS - NA_WIN_H)

    def scores(r):
        rs = win(r)
        qr = q_ref[0, r * GRID_W:(r + 1) * GRID_W, :].astype(F32) * (HEAD_DIM ** -0.5)
        kw = k_ref[0, rs * GRID_W:rs * GRID_W + NA_KEYS, :]
        q2 = jnp.concatenate([jnp.where(h0, qr, 0.0), jnp.where(h0, 0.0, qr)], axis=0).astype(BF16)
        return lax.dot_general(q2, kw, NT_DIMS, preferred_element_type=F32) + tab_ref[0, r - rs]

    pending = [scores(r) for r in range(NA_AHEAD)]
    for r in range(GRID_ROWS):
        s = pending.pop(0)
        if r + NA_AHEAD < GRID_ROWS:
            pending.append(scores(r + NA_AHEAD))
        rs = win(r)
        vw = v_ref[0, rs * GRID_W:rs * GRID_W + NA_KEYS, :]
        e = jnp.exp(s - jnp.max(s, axis=-1, keepdims=True))
        p = e * (1.0 / jnp.sum(e, axis=-1, keepdims=True))
        o = jnp.dot(p.astype(BF16), vw, preferred_element_type=F32)
        o_ref[0, r * GRID_W:(r + 1) * GRID_W, :] = jnp.where(h0, o[:GRID_W], o[GRID_W:]).astype(BF16)


def na_attention(proj, na_tab):
    return pl.pallas_call(
        _na_kernel,
        grid=(BATCH, N_NA_HEADS // 2),
        in_specs=[
            pl.BlockSpec((1, SEQ, LANES), lambda b, h: (b, 0, NQ_BLK + h)),
            pl.BlockSpec((1, SEQ, LANES), lambda b, h: (b, 0, NK_BLK + h)),
            pl.BlockSpec((1, SEQ, LANES), lambda b, h: (b, 0, NV_BLK + h)),
            pl.BlockSpec((1, NA_WIN_H, 2 * GRID_W, NA_KEYS), lambda b, h: (h, 0, 0, 0)),
        ],
        out_specs=pl.BlockSpec((1, SEQ, LANES), lambda b, h: (b, 0, h)),
        out_shape=jax.ShapeDtypeStruct((BATCH, SEQ, N_NA_HEADS * HEAD_DIM), BF16),
        compiler_params=_params(("parallel", "parallel")),
        name="na_attention",
    )(proj, proj, proj, na_tab)


LN_TM = 512


def _layer_norm(y, g, b):
    mu = jnp.mean(y, axis=-1, keepdims=True)
    yc = y - mu
    var = jnp.mean(yc * yc, axis=-1, keepdims=True)
    return yc * lax.rsqrt(var + LN_EPS) * g + b


def _out_ln_kernel(od_ref, on_ref, x_ref, w_ref, g_ref, b_ref, y_ref, yt_ref):
    half = N_DIFF_HEADS * HEAD_DIM
    h = jnp.dot(od_ref[...], w_ref[0:half, :], preferred_element_type=F32)
    h = h + jnp.dot(on_ref[...], w_ref[half:2 * half, :], preferred_element_type=F32)
    y = _layer_norm(ALPHA * x_ref[...] + h, g_ref[...], b_ref[...])
    y_ref[...] = y
    yt_ref[0] = y.T.astype(BF16)


def out_proj_ln(o_diff, o_na, x2d, w, g, b):
    half = N_DIFF_HEADS * HEAD_DIM
    return pl.pallas_call(
        _out_ln_kernel,
        grid=(TOKENS // LN_TM,),
        in_specs=[
            pl.BlockSpec((LN_TM, half), lambda i: (i, 0)),
            pl.BlockSpec((LN_TM, half), lambda i: (i, 0)),
            pl.BlockSpec((LN_TM, D_MODEL), lambda i: (i, 0)),
            pl.BlockSpec((D_MODEL, D_MODEL), lambda i: (0, 0)),
            pl.BlockSpec((1, D_MODEL), lambda i: (0, 0)),
            pl.BlockSpec((1, D_MODEL), lambda i: (0, 0)),
        ],
        out_specs=[pl.BlockSpec((LN_TM, D_MODEL), lambda i: (i, 0)),
                   pl.BlockSpec((1, D_MODEL, LN_TM), lambda i: (i, 0, 0))],
        out_shape=[jax.ShapeDtypeStruct((TOKENS, D_MODEL), F32),
                   jax.ShapeDtypeStruct((TOKENS // LN_TM, D_MODEL, LN_TM), BF16)],
        compiler_params=_params(("parallel",)),
        name="out_proj_ln",
    )(o_diff, o_na, x2d, w, g, b)


RT_TM = LN_TM
EH_ROWS = PEER_N_KEYS * PEER_HEADS


def _cmp_exchange(xs, i, l):
    a, b = xs[i], xs[l]
    xs[i], xs[l] = jnp.maximum(a, b), jnp.minimum(a, b)


def _bitonic_merge_desc(xs):
    xs = list(xs)
    j = len(xs) // 2
    while j >= 1:
        for i in range(len(xs)):
            if i & j == 0:
                _cmp_exchange(xs, i, i | j)
        j //= 2
    return xs


def _sort16_desc(xs):
    xs = list(xs)
    k = 2
    while k <= PEER_TOPK:
        j = k // 2
        while j >= 1:
            for i in range(PEER_TOPK):
                l = i ^ j
                if l > i:
                    if i & k == 0:
                        _cmp_exchange(xs, i, l)
                    else:
                        _cmp_exchange(xs, l, i)
            j //= 2
        k *= 2
    return xs


def _top16_desc(xs):
    runs = [_sort16_desc(xs[g:g + PEER_TOPK]) for g in range(0, len(xs), PEER_TOPK)]
    while len(runs) > 1:
        nxt = []
        for a, b in zip(runs[0::2], runs[1::2]):
            nxt.append(_bitonic_merge_desc([jnp.maximum(a[i], b[PEER_TOPK - 1 - i])
                                            for i in range(PEER_TOPK)]))
        runs = nxt
    return runs[0]


BIG = 3.0e38


def _route_kernel(xt_ref, wqt_ref, keh_ref, c1_ref, p1_ref, s2_ref, p2_ref, s_scr, t_scr):
    qt = jnp.dot(wqt_ref[...], xt_ref[0], preferred_element_type=F32).astype(BF16)
    for a in range(2):
        s = jnp.dot(keh_ref[a], qt[a * EH_ROWS:(a + 1) * EH_ROWS], preferred_element_type=F32)
        for b in range(RT_TM // LANES):
            s_scr[a, b] = s[:, b * LANES:(b + 1) * LANES]

    def block(b, carry):
        s1 = s_scr[0, b].reshape(PEER_N_KEYS, PEER_HEADS, LANES)
        s2 = s_scr[1, b].reshape(PEER_N_KEYS, PEER_HEADS, LANES)
        v1 = _top16_desc([s1[e] for e in range(PEER_N_KEYS)])
        v2 = _top16_desc([s2[e] for e in range(PEER_N_KEYS)])
        cands = [v1[i] + v2[j] for (i, j) in STAIR]
        pad = [jnp.full_like(cands[0], -jnp.inf)] * (-len(cands) % PEER_TOPK)
        best = _top16_desc(cands + pad)
        top, thr = best[0], best[PEER_TOPK - 1]
        z = jnp.zeros_like(top)
        cut = [jnp.full_like(top, BIG) for _ in range(PEER_TOPK)]
        for (i, j), c in zip(STAIR, cands):
            sel = c >= thr
            z = z + jnp.where(sel, jnp.exp(c - top), 0.0)
            cut[i] = jnp.where(sel, jnp.minimum(cut[i], v2[j]), cut[i])
        inv_z = 1.0 / z
        c1 = jnp.full_like(s1, BIG)
        for i in range(PEER_TOPK):
            c1 = jnp.where(s1 == v1[i][None], cut[i][None], c1)
        c1_ref[b] = c1
        p1_ref[b] = jnp.exp(s1 - v1[0][None]) * inv_z[None]
        t_scr[...] = jnp.exp(s2 - v2[0][None]).reshape(EH_ROWS, LANES)
        for h in range(PEER_HEADS):
            s2_ref[b, h] = s_scr[1, b, pl.ds(h, PEER_N_KEYS, stride=PEER_HEADS), :]
            p2_ref[b, h] = t_scr[pl.ds(h, PEER_N_KEYS, stride=PEER_HEADS), :]
        return carry

    lax.fori_loop(0, RT_TM // LANES, block, 0)


def peer_route(xt, wqt, keh):
    nblk = TOKENS // LANES
    nb = RT_TM // LANES
    shp_kh = jax.ShapeDtypeStruct((nblk, PEER_N_KEYS, PEER_HEADS, LANES), F32)
    shp_hk = jax.ShapeDtypeStruct((nblk, PEER_HEADS, PEER_N_KEYS, LANES), F32)
    ospec_kh = pl.BlockSpec((nb, PEER_N_KEYS, PEER_HEADS, LANES), lambda i: (i, 0, 0, 0))
    ospec_hk = pl.BlockSpec((nb, PEER_HEADS, PEER_N_KEYS, LANES), lambda i: (i, 0, 0, 0))
    return pl.pallas_call(
        _route_kernel,
        grid=(TOKENS // RT_TM,),
        in_specs=[
            pl.BlockSpec((1, D_MODEL, RT_TM), lambda i: (i, 0, 0)),
            pl.BlockSpec((2 * EH_ROWS, D_MODEL), lambda i: (0, 0)),
            pl.BlockSpec((2, EH_ROWS, EH_ROWS), lambda i: (0, 0, 0)),
        ],
        out_specs=[ospec_kh, ospec_kh, ospec_hk, ospec_hk],
        out_shape=[shp_kh, shp_kh, shp_hk, shp_hk],
        scratch_shapes=[pltpu.VMEM((2, nb, EH_ROWS, LANES), F32), pltpu.VMEM((EH_ROWS, LANES), F32)],
        compiler_params=_params(("parallel",)),
        name="peer_route",
    )(xt, wqt, keh)


PE_SLAB = LN_TM
PE_NSLAB = 2
PE_TM = PE_NSLAB * PE_SLAB
PE_E1 = 8
PE_TE = PE_E1 * PEER_N_KEYS


def _gelu(x):
    return x * (lax.erf(x * (1.0 / math.sqrt(2.0))) + 1.0) * 0.5


PE_TILES = PEER_N_KEYS // PE_E1
PE_E1G = 2
PE_CHUNK = 4 * PEER_N_KEYS
PE_NCH = PE_TE // PE_CHUNK
PE_DROWS = D_MODEL // PE_NCH
PE_K2 = 64
PE_LAG3 = 2


def _peer_chunk(idx, xt_ref, ut_ref, v_ref, c1_ref, p1_ref, s2_ref, p2_ref, o_ref, h_new, h_old, w_new, w_old):
    sl = idx // PE_NCH
    c = idx % PE_NCH
    c0 = pl.multiple_of(c * PE_CHUNK, PE_CHUNK)
    d0 = pl.multiple_of(c * PE_DROWS, PE_DROWS)
    for tc in range(PE_SLAB // LANES):
        tb = sl * (PE_SLAB // LANES) + tc
        ts = pl.ds(tc * LANES, LANES)
        for e0 in range(0, PE_CHUNK // PEER_N_KEYS, PE_E1G):
            for k0 in range(0, PEER_N_KEYS, PE_K2):
                g = [None] * PE_E1G
                for h in range(PEER_HEADS):
                    s2 = s2_ref[tb, h, pl.ds(k0, PE_K2), :]
                    p2 = p2_ref[tb, h, pl.ds(k0, PE_K2), :]
                    for e in range(PE_E1G):
                        el = c * (PE_CHUNK // PEER_N_KEYS) + e0 + e
                        c1 = c1_ref[tb, el, pl.ds(h, 1), :]
                        p1 = p1_ref[tb, el, pl.ds(h, 1), :]
                        t = jnp.where(s2 >= c1, p1 * p2, 0.0)
                        g[e] = t if g[e] is None else g[e] + t
                for e in range(PE_E1G):
                    rows = pl.ds(c0 + (e0 + e) * PEER_N_KEYS + k0, PE_K2)
                    w_new[sl, rows, ts] = (g[e] * _gelu(h_old[sl, rows, ts])).astype(BF16)
    h_new[sl, pl.ds(c0, PE_CHUNK), :] = lax.dot_general(ut_ref[c], xt_ref[sl], TN_DIMS,
                                                        preferred_element_type=F32)
    o_ref[sl, pl.ds(d0, PE_DROWS), :] += lax.dot_general(v_ref[c], w_old[sl], TN_DIMS,
                                                         preferred_element_type=F32)


def _peer_step(*refs):
    def body(idx, carry):
        _peer_chunk(idx, *refs)
        return carry

    lax.fori_loop(0, PE_NSLAB * PE_NCH, body, 0)


def _peer_kernel(xt_ref, ut_ref, v_ref, c1_ref, p1_ref, s2_ref, p2_ref, o_ref, h_a, h_b, w_a, w_b):
    s = pl.program_id(0)

    @pl.when(s == 0)
    def _():
        h_b[...] = jnp.zeros_like(h_b)
        w_b[...] = jnp.zeros_like(w_b)

    @pl.when((s == 0) | (s % PE_TILES == PE_LAG3))
    def _():
        o_ref[...] = jnp.zeros_like(o_ref)

    args = (xt_ref, ut_ref, v_ref, c1_ref, p1_ref, s2_ref, p2_ref, o_ref)

    @pl.when(s % 2 == 0)
    def _():
        _peer_step(*args, h_a, h_b, w_a, w_b)

    @pl.when(s % 2 == 1)
    def _():
        _peer_step(*args, h_b, h_a, w_b, w_a)


def peer_experts(xt, ut, v, c1, p1, s2, p2):
    n_items = (TOKENS // PE_TM) * PE_TILES
    nb = PE_TM // LANES

    def item(s, lag):
        w = jnp.clip(s - lag, 0, n_items - 1)
        return w // PE_TILES, w % PE_TILES

    gate_e1 = pl.BlockSpec((nb, PE_E1, PEER_HEADS, LANES), lambda s: (*item(s, 1), 0, 0))
    gate_e2 = pl.BlockSpec((nb, PEER_HEADS, PEER_N_KEYS, LANES), lambda s: (item(s, 1)[0], 0, 0, 0))
    return pl.pallas_call(
        _peer_kernel,
        grid=(n_items + PE_LAG3,),
        in_specs=[
            pl.BlockSpec((PE_NSLAB, D_MODEL, PE_SLAB), lambda s: (item(s, 0)[0], 0, 0)),
            pl.BlockSpec((PE_NCH, D_MODEL, PE_CHUNK), lambda s: (item(s, 0)[1], 0, 0)),
            pl.BlockSpec((PE_NCH, PE_TE, PE_DROWS), lambda s: (item(s, PE_LAG3)[1], 0, 0)),
            gate_e1, gate_e1, gate_e2, gate_e2,
        ],
        out_specs=pl.BlockSpec((PE_NSLAB, D_MODEL, PE_SLAB), lambda s: (item(s, PE_LAG3)[0], 0, 0)),
        out_shape=jax.ShapeDtypeStruct((TOKENS // PE_SLAB, D_MODEL, PE_SLAB), F32),
        scratch_shapes=[pltpu.VMEM((PE_NSLAB, PE_TE, PE_SLAB), F32), pltpu.VMEM((PE_NSLAB, PE_TE, PE_SLAB), F32),
                        pltpu.VMEM((PE_NSLAB, PE_TE, PE_SLAB), BF16), pltpu.VMEM((PE_NSLAB, PE_TE, PE_SLAB), BF16)],
        compiler_params=_params(("arbitrary",)),
        name="peer_experts",
    )(xt, ut, v, c1, p1, s2, p2)


def _res_ln_kernel(x_ref, ft_ref, g_ref, b_ref, y_ref):
    y_ref[...] = _layer_norm(ALPHA * x_ref[...] + ft_ref[0].T, g_ref[...], b_ref[...])


def res_ln(x2d, ft, g, b):
    return pl.pallas_call(
        _res_ln_kernel,
        grid=(TOKENS // LN_TM,),
        in_specs=[
            pl.BlockSpec((LN_TM, D_MODEL), lambda i: (i, 0)),
            pl.BlockSpec((1, D_MODEL, LN_TM), lambda i: (i, 0, 0)),
            pl.BlockSpec((1, D_MODEL), lambda i: (0, 0)),
            pl.BlockSpec((1, D_MODEL), lambda i: (0, 0)),
        ],
        out_specs=pl.BlockSpec((LN_TM, D_MODEL), lambda i: (i, 0)),
        out_shape=jax.ShapeDtypeStruct((TOKENS, D_MODEL), F32),
        compiler_params=_params(("parallel",)),
        name="res_ln",
    )(x2d, ft, g, b)


def _cast_kernel(x_ref, o_ref):
    o_ref[0] = x_ref[...].astype(BF16)


def _cast_t_kernel(x_ref, o_ref):
    o_ref[0] = x_ref[...].T.astype(BF16)


def expert_up_layout(u):
    n = PEER_N_EXPERTS // PE_CHUNK
    return pl.pallas_call(
        _cast_t_kernel,
        grid=(n,),
        in_specs=[pl.BlockSpec((PE_CHUNK, D_MODEL), lambda i: (i, 0))],
        out_specs=pl.BlockSpec((1, D_MODEL, PE_CHUNK), lambda i: (i, 0, 0)),
        out_shape=jax.ShapeDtypeStruct((n, D_MODEL, PE_CHUNK), BF16),
        compiler_params=_params(("parallel",)),
        name="expert_up_layout",
    )(u)


def expert_down_layout(v):
    return pl.pallas_call(
        _cast_kernel,
        grid=(PE_TILES, PE_NCH),
        in_specs=[pl.BlockSpec((PE_TE, PE_DROWS), lambda i, j: (i, j))],
        out_specs=pl.BlockSpec((1, PE_TE, PE_DROWS), lambda i, j: (i * PE_NCH + j, 0, 0)),
        out_shape=jax.ShapeDtypeStruct((PE_TILES * PE_NCH, PE_TE, PE_DROWS), BF16),
        compiler_params=_params(("parallel", "parallel")),
        name="expert_down_layout",
    )(v)


def query_proj_layout(wq):
    return pl.pallas_call(
        _cast_t_kernel,
        grid=(2, PEER_HEADS),
        in_specs=[pl.BlockSpec((D_MODEL, PEER_N_KEYS), lambda a, h: (0, 2 * h + a))],
        out_specs=pl.BlockSpec((1, PEER_N_KEYS, D_MODEL), lambda a, h: (a * PEER_HEADS + h, 0, 0)),
        out_shape=jax.ShapeDtypeStruct((2 * PEER_HEADS, PEER_N_KEYS, D_MODEL), BF16),
        compiler_params=_params(("parallel", "parallel")),
        name="query_proj_layout",
    )(wq).reshape(2 * EH_ROWS, D_MODEL)


def _t5_bucket(rel):
    half = T5_BUCKETS // 2
    base = jnp.where(rel > 0, half, 0)
    n = jnp.abs(rel)
    nf = jnp.maximum(n, 1).astype(F32)
    large = T5_MAX_EXACT + (jnp.log(nf / T5_MAX_EXACT) / math.log(T5_MAX_DIST / T5_MAX_EXACT)
                            * (half - T5_MAX_EXACT)).astype(jnp.int32)
    large = jnp.minimum(large, half - 1)
    return base + jnp.where(n < T5_MAX_EXACT, n, large)


def _t5_tiles(t5_bias):
    r = jnp.arange(LANES, dtype=jnp.int32)
    blk = jnp.arange(-2, 3, dtype=jnp.int32)
    rel = blk[:, None, None] * LANES + r[None, None, :] - r[None, :, None]
    onehot = (_t5_bucket(rel)[..., None] == jnp.arange(T5_BUCKETS, dtype=jnp.int32)).astype(F32)
    return jnp.einsum('tqkc,ch->htqk', onehot, t5_bias.astype(F32), precision=lax.Precision.HIGHEST)


def _na_tiles(rpb):
    c = jnp.arange(GRID_W, dtype=jnp.int32)
    col_start = jnp.clip(c - NA_WIN_W // 2, 0, GRID_W - NA_WIN_W)
    in_win = (c[None, :] >= col_start[:, None]) & (c[None, :] < col_start[:, None] + NA_WIN_W)
    dc = jnp.clip(c[None, :] - c[:, None] + (NA_WIN_W - 1), 0, 2 * NA_WIN_W - 2)
    onehot = (dc[..., None] == jnp.arange(2 * NA_WIN_W - 1, dtype=jnp.int32)).astype(F32)
    rows = jnp.stack([rpb[:, NA_WIN_H - 1 - off:2 * NA_WIN_H - 1 - off, :] for off in range(NA_WIN_H)],
                     axis=1).astype(F32)
    bias = jnp.einsum('hoic,qkc->hoqik', rows, onehot, precision=lax.Precision.HIGHEST)
    bias = jnp.where(in_win[None, None, :, None, :], bias, NEG_INF)
    bias = bias.reshape(N_NA_HEADS // 2, 2, NA_WIN_H, GRID_W, NA_KEYS)
    return bias.transpose(0, 2, 1, 3, 4).reshape(N_NA_HEADS // 2, NA_WIN_H, 2 * GRID_W, NA_KEYS)


def kernel(x, w_in, w_out, lam_q1, lam_k1, lam_q2, lam_k2, diff_norm_g, t5_bias, na_rpb,
           ln1_g, ln1_b, peer_wq, peer_keys, peer_u, peer_v, ln2_g, ln2_b):
    xf = x.reshape(TOKENS, D_MODEL)
    t5_tab = _t5_tiles(t5_bias) * LOG2E
    eye = jnp.eye(PEER_HEADS, dtype=F32)
    for l in range(DEPTH):
        lam_init = 0.8 - 0.6 * math.exp(-0.3 * l)
        lam = (jnp.exp(jnp.sum(lam_q1[l] * lam_k1[l])) - jnp.exp(jnp.sum(lam_q2[l] * lam_k2[l]))
               + lam_init).reshape(1).astype(F32)
        proj = in_proj(xf, w_in[l].astype(BF16)).reshape(BATCH, SEQ, IN_PROJ_W)
        g2 = jnp.tile(diff_norm_g[l], 2).reshape(1, LANES)
        o_diff = diff_attention(proj, lam, t5_tab, g2, lam_init)
        o_na = na_attention(proj, _na_tiles(na_rpb[l]))
        x1, x1t = out_proj_ln(o_diff.reshape(TOKENS, -1), o_na.reshape(TOKENS, -1), xf,
                              w_out[l].astype(BF16), ln1_g[l].reshape(1, -1), ln1_b[l].reshape(1, -1))
        keh = jnp.einsum('aed,hg->aehgd', peer_keys[l], eye).reshape(2, EH_ROWS, EH_ROWS).astype(BF16)
        c1, p1, s2, p2 = peer_route(x1t, query_proj_layout(peer_wq[l]), keh)
        ft = peer_experts(x1t, expert_up_layout(peer_u[l]), expert_down_layout(peer_v[l]), c1, p1, s2, p2)
        xf = res_ln(x1, ft, ln2_g[l].reshape(1, -1), ln2_b[l].reshape(1, -1))
    return xf.reshape(BATCH, SEQ, D_MODEL)
```

```python
import functools
import math

import jax
import jax.numpy as jnp
from jax import lax
from jax.experimental import pallas as pl
from jax.experimental.pallas import tpu as pltpu

F32 = jnp.float32
BF16 = jnp.bfloat16

D_MODEL = 1024
BATCH = 8
SEQ = 2048
TOKENS = BATCH * SEQ
DEPTH = 2
HEAD_DIM = 64
N_DIFF_HEADS = 8
DIFF_QK_DIM = 32
N_NA_HEADS = 8
IN_PROJ_W = 3072
GRID_W = 64
GRID_ROWS = SEQ // GRID_W
NA_WIN_H = 8
NA_WIN_W = 16
NA_KEYS = NA_WIN_H * GRID_W
T5_BUCKETS = 32
T5_MAX_EXACT = 8
T5_MAX_DIST = 128
PEER_HEADS = 8
PEER_N_KEYS = 128
PEER_N_EXPERTS = PEER_N_KEYS * PEER_N_KEYS
PEER_TOPK = 16
LN_EPS = 1e-5
NEG_INF = -1e30
ALPHA = (2 * DEPTH) ** 0.25
LOG2E = 1.4426950408889634

LANES = 128
VMEM_LIMIT = 56 * 1024 * 1024

DQ_BLK, DK_BLK, DV_BLK, NQ_BLK, NK_BLK, NV_BLK = 0, 4, 8, 12, 16, 20

STAIR = [(i, j) for i in range(PEER_TOPK) for j in range(PEER_TOPK)
         if (i + 1) * (j + 1) <= PEER_TOPK]

NT_DIMS = (((1,), (1,)), ((), ()))
TN_DIMS = (((0,), (0,)), ((), ()))


def _params(sem):
    return pltpu.CompilerParams(dimension_semantics=sem, vmem_limit_bytes=VMEM_LIMIT)


IN_TM = 512


def _in_proj_kernel(x_ref, w_ref, o_ref):
    o_ref[...] = jnp.dot(x_ref[...].astype(BF16), w_ref[...],
                         preferred_element_type=F32).astype(BF16)


def in_proj(x2d, w):
    return pl.pallas_call(
        _in_proj_kernel,
        grid=(TOKENS // IN_TM,),
        in_specs=[pl.BlockSpec((IN_TM, D_MODEL), lambda i: (i, 0)),
                  pl.BlockSpec((D_MODEL, IN_PROJ_W), lambda i: (0, 0))],
        out_specs=pl.BlockSpec((IN_TM, IN_PROJ_W), lambda i: (i, 0)),
        out_shape=jax.ShapeDtypeStruct((TOKENS, IN_PROJ_W), BF16),
        compiler_params=_params(("parallel",)),
        name="in_proj",
    )(x2d, w)


DIFF_TQ = 512
KEY_BLOCKS = SEQ // LANES


def _diff_kernel(lam_ref, q_ref, k_ref, v_ref, tab_ref, g_ref, o_ref, *, lam_init):
    qi = pl.program_id(2)
    lam = lam_ref[0]
    q = q_ref[0].astype(F32) * (DIFF_QK_DIM ** -0.5 * LOG2E)
    k = k_ref[0]
    v = v_ref[0]
    lane = lax.broadcasted_iota(jnp.int32, (1, LANES), 1)

    def scores(hh):
        rows = []
        for sb in range(DIFF_TQ // LANES):
            qblk = qi * (DIFF_TQ // LANES) + sb
            tiles = [tab_ref[hh, jnp.clip(kj - qblk, -2, 2) + 2] for kj in range(KEY_BLOCKS)]
            rows.append(jnp.concatenate(tiles, axis=1))
        bias = jnp.concatenate(rows, axis=0)
        out = []
        for m in range(2):
            c0 = hh * HEAD_DIM + m * DIFF_QK_DIM
            msk = (lane >= c0) & (lane < c0 + DIFF_QK_DIM)
            qs = jnp.where(msk, q, 0.0).astype(BF16)
            out.append(lax.dot_general(qs, k, NT_DIMS, preferred_element_type=F32) + bias)
        return out

    def weights(ss):
        es = [jnp.exp2(s - jnp.max(s, axis=-1, keepdims=True)) for s in ss]
        zs = [jnp.sum(e, axis=-1, keepdims=True) for e in es]
        return (es[0] - (lam * zs[0] / zs[1]) * es[1]).astype(BF16), 1.0 / zs[0]

    def head_out(hh, w, rz):
        o = jnp.dot(w, v, preferred_element_type=F32) * rz
        hm = (lane >= hh * HEAD_DIM) & (lane < (hh + 1) * HEAD_DIM)
        ms = jnp.sum(jnp.where(hm, o * o, 0.0), axis=-1, keepdims=True) * (1.0 / HEAD_DIM)
        return jnp.where(hm, o * lax.rsqrt(ms + LN_EPS), 0.0)

    s0 = scores(0)
    s1 = scores(1)
    w0, rz0 = weights(s0)
    o0 = head_out(0, w0, rz0)
    w1, rz1 = weights(s1)
    o1 = head_out(1, w1, rz1)
    o_ref[0] = ((o0 + o1) * g_ref[...] * (1.0 - lam_init)).astype(BF16)


def diff_attention(proj, lam, t5_tab, norm_g2, lam_init):
    nq = SEQ // DIFF_TQ
    return pl.pallas_call(
        functools.partial(_diff_kernel, lam_init=lam_init),
        grid=(BATCH, N_DIFF_HEADS // 2, nq),
        in_specs=[
            pl.BlockSpec(memory_space=pltpu.SMEM),
            pl.BlockSpec((1, DIFF_TQ, LANES), lambda b, h, i: (b, i, DQ_BLK + h)),
            pl.BlockSpec((1, SEQ, LANES), lambda b, h, i: (b, 0, DK_BLK + h)),
            pl.BlockSpec((1, SEQ, LANES), lambda b, h, i: (b, 0, DV_BLK + h)),
            pl.BlockSpec((2, 5, LANES, LANES), lambda b, h, i: (h, 0, 0, 0)),
            pl.BlockSpec((1, LANES), lambda b, h, i: (0, 0)),
        ],
        out_specs=pl.BlockSpec((1, DIFF_TQ, LANES), lambda b, h, i: (b, i, h)),
        out_shape=jax.ShapeDtypeStruct((BATCH, SEQ, N_DIFF_HEADS * HEAD_DIM), BF16),
        compiler_params=_params(("parallel", "parallel", "arbitrary")),
        name="diff_attention",
    )(lam, proj, proj, proj, t5_tab, norm_g2)


NA_AHEAD = 4


def _na_kernel(q_ref, k_ref, v_ref, tab_ref, o_ref):
    lane = lax.broadcasted_iota(jnp.int32, (1, LANES), 1)
    h0 = lane < HEAD_DIM

    def win(r):
        return min(max(r - NA_WIN_H // 2, 0), GRID_ROWS - NA_WIN_H)

    def scores(r):
        rs = win(r)
        qr = q_ref[0, r * GRID_W:(r + 1) * GRID_W, :].astype(F32) * (HEAD_DIM ** -0.5)
        kw = k_ref[0, rs * GRID_W:rs * GRID_W + NA_KEYS, :]
        q2 = jnp.concatenate([jnp.where(h0, qr, 0.0), jnp.where(h0, 0.0, qr)], axis=0).astype(BF16)
        return lax.dot_general(q2, kw, NT_DIMS, preferred_element_type=F32) + tab_ref[0, r - rs]

    pending = [scores(r) for r in range(NA_AHEAD)]
    for r in range(GRID_ROWS):
        s = pending.pop(0)
        if r + NA_AHEAD < GRID_ROWS:
            pending.append(scores(r + NA_AHEAD))
        rs = win(r)
        vw = v_ref[0, rs * GRID_W:rs * GRID_W + NA_KEYS, :]
        e = jnp.exp(s - jnp.max(s, axis=-1, keepdims=True))
        p = e * (1.0 / jnp.sum(e, axis=-1, keepdims=True))
        o = jnp.dot(p.astype(BF16), vw, preferred_element_type=F32)
        o_ref[0, r * GRID_W:(r + 1) * GRID_W, :] = jnp.where(h0, o[:GRID_W], o[GRID_W:]).astype(BF16)


def na_attention(proj, na_tab):
    return pl.pallas_call(
        _na_kernel,
        grid=(BATCH, N_NA_HEADS // 2),
        in_specs=[
            pl.BlockSpec((1, SEQ, LANES), lambda b, h: (b, 0, NQ_BLK + h)),
            pl.BlockSpec((1, SEQ, LANES), lambda b, h: (b, 0, NK_BLK + h)),
            pl.BlockSpec((1, SEQ, LANES), lambda b, h: (b, 0, NV_BLK + h)),
            pl.BlockSpec((1, NA_WIN_H, 2 * GRID_W, NA_KEYS), lambda b, h: (h, 0, 0, 0)),
        ],
        out_specs=pl.BlockSpec((1, SEQ, LANES), lambda b, h: (b, 0, h)),
        out_shape=jax.ShapeDtypeStruct((BATCH, SEQ, N_NA_HEADS * HEAD_DIM), BF16),
        compiler_params=_params(("parallel", "parallel")),
        name="na_attention",
    )(proj, proj, proj, na_tab)


LN_TM = 512


def _layer_norm(y, g, b):
    mu = jnp.mean(y, axis=-1, keepdims=True)
    yc = y - mu
    var = jnp.mean(yc * yc, axis=-1, keepdims=True)
    return yc * lax.rsqrt(var + LN_EPS) * g + b


def _out_ln_kernel(od_ref, on_ref, x_ref, w_ref, g_ref, b_ref, y_ref, yt_ref):
    half = N_DIFF_HEADS * HEAD_DIM
    h = jnp.dot(od_ref[...], w_ref[0:half, :], preferred_element_type=F32)
    h = h + jnp.dot(on_ref[...], w_ref[half:2 * half, :], preferred_element_type=F32)
    y = _layer_norm(ALPHA * x_ref[...] + h, g_ref[...], b_ref[...])
    y_ref[...] = y
    yt_ref[0] = y.T.astype(BF16)


def out_proj_ln(o_diff, o_na, x2d, w, g, b):
    half = N_DIFF_HEADS * HEAD_DIM
    return pl.pallas_call(
        _out_ln_kernel,
        grid=(TOKENS // LN_TM,),
        in_specs=[
            pl.BlockSpec((LN_TM, half), lambda i: (i, 0)),
            pl.BlockSpec((LN_TM, half), lambda i: (i, 0)),
            pl.BlockSpec((LN_TM, D_MODEL), lambda i: (i, 0)),
            pl.BlockSpec((D_MODEL, D_MODEL), lambda i: (0, 0)),
            pl.BlockSpec((1, D_MODEL), lambda i: (0, 0)),
            pl.BlockSpec((1, D_MODEL), lambda i: (0, 0)),
        ],
        out_specs=[pl.BlockSpec((LN_TM, D_MODEL), lambda i: (i, 0)),
                   pl.BlockSpec((1, D_MODEL, LN_TM), lambda i: (i, 0, 0))],
        out_shape=[jax.ShapeDtypeStruct((TOKENS, D_MODEL), F32),
                   jax.ShapeDtypeStruct((TOKENS // LN_TM, D_MODEL, LN_TM), BF16)],
        compiler_params=_params(("parallel",)),
        name="out_proj_ln",
    )(o_diff, o_na, x2d, w, g, b)


RT_TM = LN_TM
EH_ROWS = PEER_N_KEYS * PEER_HEADS


def _cmp_exchange(xs, i, l):
    a, b = xs[i], xs[l]
    xs[i], xs[l] = jnp.maximum(a, b), jnp.minimum(a, b)


def _bitonic_merge_desc(xs):
    xs = list(xs)
    j = len(xs) // 2
    while j >= 1:
        for i in range(len(xs)):
            if i & j == 0:
                _cmp_exchange(xs, i, i | j)
        j //= 2
    return xs


def _sort16_desc(xs):
    xs = list(xs)
    k = 2
    while k <= PEER_TOPK:
        j = k // 2
        while j >= 1:
            for i in range(PEER_TOPK):
                l = i ^ j
                if l > i:
                    if i & k == 0:
                        _cmp_exchange(xs, i, l)
                    else:
                        _cmp_exchange(xs, l, i)
            j //= 2
        k *= 2
    return xs


def _top16_desc(xs):
    runs = [_sort16_desc(xs[g:g + PEER_TOPK]) for g in range(0, len(xs), PEER_TOPK)]
    while len(runs) > 1:
        nxt = []
        for a, b in zip(runs[0::2], runs[1::2]):
            nxt.append(_bitonic_merge_desc([jnp.maximum(a[i], b[PEER_TOPK - 1 - i])
                                            for i in range(PEER_TOPK)]))
        runs = nxt
    return runs[0]


BIG = 3.0e38


def _route_kernel(xt_ref, wqt_ref, keh_ref, c1_ref, p1_ref, s2_ref, p2_ref, s_scr, t_scr):
    qt = jnp.dot(wqt_ref[...], xt_ref[0], preferred_element_type=F32).astype(BF16)
    for a in range(2):
        s = jnp.dot(keh_ref[a], qt[a * EH_ROWS:(a + 1) * EH_ROWS], preferred_element_type=F32)
        for b in range(RT_TM // LANES):
            s_scr[a, b] = s[:, b * LANES:(b + 1) * LANES]

    def block(b, carry):
        s1 = s_scr[0, b].reshape(PEER_N_KEYS, PEER_HEADS, LANES)
        s2 = s_scr[1, b].reshape(PEER_N_KEYS, PEER_HEADS, LANES)
        v1 = _top16_desc([s1[e] for e in range(PEER_N_KEYS)])
        v2 = _top16_desc([s2[e] for e in range(PEER_N_KEYS)])
        cands = [v1[i] + v2[j] for (i, j) in STAIR]
        pad = [jnp.full_like(cands[0], -jnp.inf)] * (-len(cands) % PEER_TOPK)
        best = _top16_desc(cands + pad)
        top, thr = best[0], best[PEER_TOPK - 1]
        z = jnp.zeros_like(top)
        cut = [jnp.full_like(top, BIG) for _ in range(PEER_TOPK)]
        for (i, j), c in zip(STAIR, cands):
            sel = c >= thr
            z = z + jnp.where(sel, jnp.exp(c - top), 0.0)
            cut[i] = jnp.where(sel, jnp.minimum(cut[i], v2[j]), cut[i])
        inv_z = 1.0 / z
        c1 = jnp.full_like(s1, BIG)
        for i in range(PEER_TOPK):
            c1 = jnp.where(s1 == v1[i][None], cut[i][None], c1)
        c1_ref[b] = c1
        p1_ref[b] = jnp.exp(s1 - v1[0][None]) * inv_z[None]
        t_scr[...] = jnp.exp(s2 - v2[0][None]).reshape(EH_ROWS, LANES)
        for h in range(PEER_HEADS):
            s2_ref[b, h] = s_scr[1, b, pl.ds(h, PEER_N_KEYS, stride=PEER_HEADS), :]
            p2_ref[b, h] = t_scr[pl.ds(h, PEER_N_KEYS, stride=PEER_HEADS), :]
        return carry

    lax.fori_loop(0, RT_TM // LANES, block, 0)


def peer_route(xt, wqt, keh):
    nblk = TOKENS // LANES
    nb = RT_TM // LANES
    shp_kh = jax.ShapeDtypeStruct((nblk, PEER_N_KEYS, PEER_HEADS, LANES), F32)
    shp_hk = jax.ShapeDtypeStruct((nblk, PEER_HEADS, PEER_N_KEYS, LANES), F32)
    ospec_kh = pl.BlockSpec((nb, PEER_N_KEYS, PEER_HEADS, LANES), lambda i: (i, 0, 0, 0))
    ospec_hk = pl.BlockSpec((nb, PEER_HEADS, PEER_N_KEYS, LANES), lambda i: (i, 0, 0, 0))
    return pl.pallas_call(
        _route_kernel,
        grid=(TOKENS // RT_TM,),
        in_specs=[
            pl.BlockSpec((1, D_MODEL, RT_TM), lambda i: (i, 0, 0)),
            pl.BlockSpec((2 * EH_ROWS, D_MODEL), lambda i: (0, 0)),
            pl.BlockSpec((2, EH_ROWS, EH_ROWS), lambda i: (0, 0, 0)),
        ],
        out_specs=[ospec_kh, ospec_kh, ospec_hk, ospec_hk],
        out_shape=[shp_kh, shp_kh, shp_hk, shp_hk],
        scratch_shapes=[pltpu.VMEM((2, nb, EH_ROWS, LANES), F32), pltpu.VMEM((EH_ROWS, LANES), F32)],
        compiler_params=_params(("parallel",)),
        name="peer_route",
    )(xt, wqt, keh)


PE_SLAB = LN_TM
PE_NSLAB = 2
PE_TM = PE_NSLAB * PE_SLAB
PE_E1 = 8
PE_TE = PE_E1 * PEER_N_KEYS


def _gelu(x):
    return x * (lax.erf(x * (1.0 / math.sqrt(2.0))) + 1.0) * 0.5


PE_TILES = PEER_N_KEYS // PE_E1
PE_E1G = 2
PE_CHUNK = 4 * PEER_N_KEYS
PE_NCH = PE_TE // PE_CHUNK
PE_DROWS = D_MODEL // PE_NCH
PE_K2 = 64
PE_LAG3 = 2


def _peer_chunk(idx, xt_ref, ut_ref, v_ref, c1_ref, p1_ref, s2_ref, p2_ref, o_ref, h_new, h_old, w_new, w_old):
    sl = idx // PE_NCH
    c = idx % PE_NCH
    c0 = pl.multiple_of(c * PE_CHUNK, PE_CHUNK)
    d0 = pl.multiple_of(c * PE_DROWS, PE_DROWS)
    for tc in range(PE_SLAB // LANES):
        tb = sl * (PE_SLAB // LANES) + tc
        ts = pl.ds(tc * LANES, LANES)
        for e0 in range(0, PE_CHUNK // PEER_N_KEYS, PE_E1G):
            for k0 in range(0, PEER_N_KEYS, PE_K2):
                g = [None] * PE_E1G
                for h in range(PEER_HEADS):
                    s2 = s2_ref[tb, h, pl.ds(k0, PE_K2), :]
                    p2 = p2_ref[tb, h, pl.ds(k0, PE_K2), :]
                    for e in range(PE_E1G):
                        el = c * (PE_CHUNK // PEER_N_KEYS) + e0 + e
                        c1 = c1_ref[tb, el, pl.ds(h, 1), :]
                        p1 = p1_ref[tb, el, pl.ds(h, 1), :]
                        t = jnp.where(s2 >= c1, p1 * p2, 0.0)
                        g[e] = t if g[e] is None else g[e] + t
                for e in range(PE_E1G):
                    rows = pl.ds(c0 + (e0 + e) * PEER_N_KEYS + k0, PE_K2)
                    w_new[sl, rows, ts] = (g[e] * _gelu(h_old[sl, rows, ts])).astype(BF16)
    h_new[sl, pl.ds(c0, PE_CHUNK), :] = lax.dot_general(ut_ref[c], xt_ref[sl], TN_DIMS,
                                                        preferred_element_type=F32)
    o_ref[sl, pl.ds(d0, PE_DROWS), :] += lax.dot_general(v_ref[c], w_old[sl], TN_DIMS,
                                                         preferred_element_type=F32)


def _peer_step(*refs):
    def body(idx, carry):
        _peer_chunk(idx, *refs)
        return carry

    lax.fori_loop(0, PE_NSLAB * PE_NCH, body, 0)


def _peer_kernel(xt_ref, ut_ref, v_ref, c1_ref, p1_ref, s2_ref, p2_ref, o_ref, h_a, h_b, w_a, w_b):
    s = pl.program_id(0)

    @pl.when(s == 0)
    def _():
        h_b[...] = jnp.zeros_like(h_b)
        w_b[...] = jnp.zeros_like(w_b)

    @pl.when((s == 0) | (s % PE_TILES == PE_LAG3))
    def _():
        o_ref[...] = jnp.zeros_like(o_ref)

    args = (xt_ref, ut_ref, v_ref, c1_ref, p1_ref, s2_ref, p2_ref, o_ref)

    @pl.when(s % 2 == 0)
    def _():
        _peer_step(*args, h_a, h_b, w_a, w_b)

    @pl.when(s % 2 == 1)
    def _():
        _peer_step(*args, h_b, h_a, w_b, w_a)


def peer_experts(xt, ut, v, c1, p1, s2, p2):
    n_items = (TOKENS // PE_TM) * PE_TILES
    nb = PE_TM // LANES

    def item(s, lag):
        w = jnp.clip(s - lag, 0, n_items - 1)
        return w // PE_TILES, w % PE_TILES

    gate_e1 = pl.BlockSpec((nb, PE_E1, PEER_HEADS, LANES), lambda s: (*item(s, 1), 0, 0))
    gate_e2 = pl.BlockSpec((nb, PEER_HEADS, PEER_N_KEYS, LANES), lambda s: (item(s, 1)[0], 0, 0, 0))
    return pl.pallas_call(
        _peer_kernel,
        grid=(n_items + PE_LAG3,),
        in_specs=[
            pl.BlockSpec((PE_NSLAB, D_MODEL, PE_SLAB), lambda s: (item(s, 0)[0], 0, 0)),
            pl.BlockSpec((PE_NCH, D_MODEL, PE_CHUNK), lambda s: (item(s, 0)[1], 0, 0)),
            pl.BlockSpec((PE_NCH, PE_TE, PE_DROWS), lambda s: (item(s, PE_LAG3)[1], 0, 0)),
            gate_e1, gate_e1, gate_e2, gate_e2,
        ],
        out_specs=pl.BlockSpec((PE_NSLAB, D_MODEL, PE_SLAB), lambda s: (item(s, PE_LAG3)[0], 0, 0)),
        out_shape=jax.ShapeDtypeStruct((TOKENS // PE_SLAB, D_MODEL, PE_SLAB), F32),
        scratch_shapes=[pltpu.VMEM((PE_NSLAB, PE_TE, PE_SLAB), F32), pltpu.VMEM((PE_NSLAB, PE_TE, PE_SLAB), F32),
                        pltpu.VMEM((PE_NSLAB, PE_TE, PE_SLAB), BF16), pltpu.VMEM((PE_NSLAB, PE_TE, PE_SLAB), BF16)],
        compiler_params=_params(("arbitrary",)),
        name="peer_experts",
    )(xt, ut, v, c1, p1, s2, p2)


def _res_ln_kernel(x_ref, ft_ref, g_ref, b_ref, y_ref):
    y_ref[...] = _layer_norm(ALPHA * x_ref[...] + ft_ref[0].T, g_ref[...], b_ref[...])


def res_ln(x2d, ft, g, b):
    return pl.pallas_call(
        _res_ln_kernel,
        grid=(TOKENS // LN_TM,),
        in_specs=[
            pl.BlockSpec((LN_TM, D_MODEL), lambda i: (i, 0)),
            pl.BlockSpec((1, D_MODEL, LN_TM), lambda i: (i, 0, 0)),
            pl.BlockSpec((1, D_MODEL), lambda i: (0, 0)),
            pl.BlockSpec((1, D_MODEL), lambda i: (0, 0)),
        ],
        out_specs=pl.BlockSpec((LN_TM, D_MODEL), lambda i: (i, 0)),
        out_shape=jax.ShapeDtypeStruct((TOKENS, D_MODEL), F32),
        compiler_params=_params(("parallel",)),
        name="res_ln",
    )(x2d, ft, g, b)


def _cast_kernel(x_ref, o_ref):
    o_ref[0] = x_ref[...].astype(BF16)


def _cast_t_kernel(x_ref, o_ref):
    o_ref[0] = x_ref[...].T.astype(BF16)


def expert_up_layout(u, l):
    n = PEER_N_EXPERTS // PE_CHUNK
    return pl.pallas_call(
        _cast_t_kernel,
        grid=(n,),
        in_specs=[pl.BlockSpec((None, PE_CHUNK, D_MODEL), lambda i: (l, i, 0))],
        out_specs=pl.BlockSpec((1, D_MODEL, PE_CHUNK), lambda i: (i, 0, 0)),
        out_shape=jax.ShapeDtypeStruct((n, D_MODEL, PE_CHUNK), BF16),
        compiler_params=_params(("parallel",)),
        name="expert_up_layout",
    )(u)


def expert_down_layout(v, l):
    return pl.pallas_call(
        _cast_kernel,
        grid=(PE_TILES, PE_NCH),
        in_specs=[pl.BlockSpec((None, PE_TE, PE_DROWS), lambda i, j: (l, i, j))],
        out_specs=pl.BlockSpec((1, PE_TE, PE_DROWS), lambda i, j: (i * PE_NCH + j, 0, 0)),
        out_shape=jax.ShapeDtypeStruct((PE_TILES * PE_NCH, PE_TE, PE_DROWS), BF16),
        compiler_params=_params(("parallel", "parallel")),
        name="expert_down_layout",
    )(v)


def query_proj_layout(wq, l):
    return pl.pallas_call(
        _cast_t_kernel,
        grid=(2, PEER_HEADS),
        in_specs=[pl.BlockSpec((None, D_MODEL, PEER_N_KEYS), lambda a, h: (l, 0, 2 * h + a))],
        out_specs=pl.BlockSpec((1, PEER_N_KEYS, D_MODEL), lambda a, h: (a * PEER_HEADS + h, 0, 0)),
        out_shape=jax.ShapeDtypeStruct((2 * PEER_HEADS, PEER_N_KEYS, D_MODEL), BF16),
        compiler_params=_params(("parallel", "parallel")),
        name="query_proj_layout",
    )(wq).reshape(2 * EH_ROWS, D_MODEL)


def _t5_bucket(rel):
    half = T5_BUCKETS // 2
    base = jnp.where(rel > 0, half, 0)
    n = jnp.abs(rel)
    nf = jnp.maximum(n, 1).astype(F32)
    large = T5_MAX_EXACT + (jnp.log(nf / T5_MAX_EXACT) / math.log(T5_MAX_DIST / T5_MAX_EXACT)
                            * (half - T5_MAX_EXACT)).astype(jnp.int32)
    large = jnp.minimum(large, half - 1)
    return base + jnp.where(n < T5_MAX_EXACT, n, large)


def _t5_tiles(t5_bias):
    r = jnp.arange(LANES, dtype=jnp.int32)
    blk = jnp.arange(-2, 3, dtype=jnp.int32)
    rel = blk[:, None, None] * LANES + r[None, None, :] - r[None, :, None]
    onehot = (_t5_bucket(rel)[..., None] == jnp.arange(T5_BUCKETS, dtype=jnp.int32)).astype(F32)
    return jnp.einsum('tqkc,ch->htqk', onehot, t5_bias.astype(F32), precision=lax.Precision.HIGHEST)


def _na_tiles(rpb):
    c = jnp.arange(GRID_W, dtype=jnp.int32)
    col_start = jnp.clip(c - NA_WIN_W // 2, 0, GRID_W - NA_WIN_W)
    in_win = (c[None, :] >= col_start[:, None]) & (c[None, :] < col_start[:, None] + NA_WIN_W)
    dc = jnp.clip(c[None, :] - c[:, None] + (NA_WIN_W - 1), 0, 2 * NA_WIN_W - 2)
    onehot = (dc[..., None] == jnp.arange(2 * NA_WIN_W - 1, dtype=jnp.int32)).astype(F32)
    rows = jnp.stack([rpb[:, NA_WIN_H - 1 - off:2 * NA_WIN_H - 1 - off, :] for off in range(NA_WIN_H)],
                     axis=1).astype(F32)
    bias = jnp.einsum('hoic,qkc->hoqik', rows, onehot, precision=lax.Precision.HIGHEST)
    bias = jnp.where(in_win[None, None, :, None, :], bias, NEG_INF)
    bias = bias.reshape(N_NA_HEADS // 2, 2, NA_WIN_H, GRID_W, NA_KEYS)
    return bias.transpose(0, 2, 1, 3, 4).reshape(N_NA_HEADS // 2, NA_WIN_H, 2 * GRID_W, NA_KEYS)


def kernel(x, w_in, w_out, lam_q1, lam_k1, lam_q2, lam_k2, diff_norm_g, t5_bias, na_rpb,
           ln1_g, ln1_b, peer_wq, peer_keys, peer_u, peer_v, ln2_g, ln2_b):
    xf = x.reshape(TOKENS, D_MODEL)
    t5_tab = _t5_tiles(t5_bias) * LOG2E
    eye = jnp.eye(PEER_HEADS, dtype=F32)
    for l in range(DEPTH):
        lam_init = 0.8 - 0.6 * math.exp(-0.3 * l)
        lam = (jnp.exp(jnp.sum(lam_q1[l] * lam_k1[l])) - jnp.exp(jnp.sum(lam_q2[l] * lam_k2[l]))
               + lam_init).reshape(1).astype(F32)
        proj = in_proj(xf, w_in[l].astype(BF16)).reshape(BATCH, SEQ, IN_PROJ_W)
        g2 = jnp.tile(diff_norm_g[l], 2).reshape(1, LANES)
        o_diff = diff_attention(proj, lam, t5_tab, g2, lam_init)
        o_na = na_attention(proj, _na_tiles(na_rpb[l]))
        x1, x1t = out_proj_ln(o_diff.reshape(TOKENS, -1), o_na.reshape(TOKENS, -1), xf,
                              w_out[l].astype(BF16), ln1_g[l].reshape(1, -1), ln1_b[l].reshape(1, -1))
        keh = jnp.einsum('aed,hg->aehgd', peer_keys[l], eye).reshape(2, EH_ROWS, EH_ROWS).astype(BF16)
        c1, p1, s2, p2 = peer_route(x1t, query_proj_layout(peer_wq, l), keh)
        ft = peer_experts(x1t, expert_up_layout(peer_u, l), expert_down_layout(peer_v, l), c1, p1, s2, p2)
        xf = res_ln(x1, ft, ln2_g[l].reshape(1, -1), ln2_b[l].reshape(1, -1))
    return xf.reshape(BATCH, SEQ, D_MODEL)
```

```python
import functools
import math

import jax
import jax.numpy as jnp
from jax import lax
from jax.experimental import pallas as pl
from jax.experimental.pallas import tpu as pltpu

F32 = jnp.float32
BF16 = jnp.bfloat16

D_MODEL = 1024
BATCH = 8
SEQ = 2048
TOKENS = BATCH * SEQ
DEPTH = 2
HEAD_DIM = 64
N_DIFF_HEADS = 8
DIFF_QK_DIM = 32
N_NA_HEADS = 8
IN_PROJ_W = 3072
GRID_W = 64
GRID_ROWS = SEQ // GRID_W
NA_WIN_H = 8
NA_WIN_W = 16
NA_KEYS = NA_WIN_H * GRID_W
T5_BUCKETS = 32
T5_MAX_EXACT = 8
T5_MAX_DIST = 128
PEER_HEADS = 8
PEER_N_KEYS = 128
PEER_N_EXPERTS = PEER_N_KEYS * PEER_N_KEYS
PEER_TOPK = 16
LN_EPS = 1e-5
NEG_INF = -1e30
ALPHA = (2 * DEPTH) ** 0.25
LOG2E = 1.4426950408889634

LANES = 128
VMEM_LIMIT = 56 * 1024 * 1024

DQ_BLK, DK_BLK, DV_BLK, NQ_BLK, NK_BLK, NV_BLK = 0, 4, 8, 12, 16, 20

STAIR = [(i, j) for i in range(PEER_TOPK) for j in range(PEER_TOPK)
         if (i + 1) * (j + 1) <= PEER_TOPK]

NT_DIMS = (((1,), (1,)), ((), ()))
TN_DIMS = (((0,), (0,)), ((), ()))


def _params(sem):
    return pltpu.CompilerParams(dimension_semantics=sem, vmem_limit_bytes=VMEM_LIMIT)


IN_TM = 512


def _in_proj_kernel(x_ref, w_ref, o_ref):
    o_ref[...] = jnp.dot(x_ref[...].astype(BF16), w_ref[...],
                         preferred_element_type=F32).astype(BF16)


def in_proj(x2d, w):
    return pl.pallas_call(
        _in_proj_kernel,
        grid=(TOKENS // IN_TM,),
        in_specs=[pl.BlockSpec((IN_TM, D_MODEL), lambda i: (i, 0)),
                  pl.BlockSpec((D_MODEL, IN_PROJ_W), lambda i: (0, 0))],
        out_specs=pl.BlockSpec((IN_TM, IN_PROJ_W), lambda i: (i, 0)),
        out_shape=jax.ShapeDtypeStruct((TOKENS, IN_PROJ_W), BF16),
        compiler_params=_params(("parallel",)),
        name="in_proj",
    )(x2d, w)


DIFF_TQ = 1024
KEY_BLOCKS = SEQ // LANES


def _diff_kernel(lam_ref, q_ref, k_ref, v_ref, tab_ref, g_ref, o_ref, *, lam_init):
    qi = pl.program_id(2)
    lam = lam_ref[0]
    q = q_ref[0].astype(F32) * (DIFF_QK_DIM ** -0.5 * LOG2E)
    k = k_ref[0]
    v = v_ref[0]
    lane = lax.broadcasted_iota(jnp.int32, (1, LANES), 1)

    def scores(hh):
        rows = []
        for sb in range(DIFF_TQ // LANES):
            qblk = qi * (DIFF_TQ // LANES) + sb
            tiles = [tab_ref[hh, jnp.clip(kj - qblk, -2, 2) + 2] for kj in range(KEY_BLOCKS)]
            rows.append(jnp.concatenate(tiles, axis=1))
        bias = jnp.concatenate(rows, axis=0)
        out = []
        for m in range(2):
            c0 = hh * HEAD_DIM + m * DIFF_QK_DIM
            msk = (lane >= c0) & (lane < c0 + DIFF_QK_DIM)
            qs = jnp.where(msk, q, 0.0).astype(BF16)
            out.append(lax.dot_general(qs, k, NT_DIMS, preferred_element_type=F32) + bias)
        return out

    def weights(ss):
        es = [jnp.exp2(s - jnp.max(s, axis=-1, keepdims=True)) for s in ss]
        zs = [jnp.sum(e, axis=-1, keepdims=True) for e in es]
        return (es[0] - (lam * zs[0] / zs[1]) * es[1]).astype(BF16), 1.0 / zs[0]

    def head_out(hh, w, rz):
        o = jnp.dot(w, v, preferred_element_type=F32) * rz
        hm = (lane >= hh * HEAD_DIM) & (lane < (hh + 1) * HEAD_DIM)
        ms = jnp.sum(jnp.where(hm, o * o, 0.0), axis=-1, keepdims=True) * (1.0 / HEAD_DIM)
        return jnp.where(hm, o * lax.rsqrt(ms + LN_EPS), 0.0)

    s0 = scores(0)
    s1 = scores(1)
    w0, rz0 = weights(s0)
    o0 = head_out(0, w0, rz0)
    w1, rz1 = weights(s1)
    o1 = head_out(1, w1, rz1)
    o_ref[0] = ((o0 + o1) * g_ref[...] * (1.0 - lam_init)).astype(BF16)


def diff_attention(proj, lam, t5_tab, norm_g2, lam_init):
    nq = SEQ // DIFF_TQ
    return pl.pallas_call(
        functools.partial(_diff_kernel, lam_init=lam_init),
        grid=(BATCH, N_DIFF_HEADS // 2, nq),
        in_specs=[
            pl.BlockSpec(memory_space=pltpu.SMEM),
            pl.BlockSpec((1, DIFF_TQ, LANES), lambda b, h, i: (b, i, DQ_BLK + h)),
            pl.BlockSpec((1, SEQ, LANES), lambda b, h, i: (b, 0, DK_BLK + h)),
            pl.BlockSpec((1, SEQ, LANES), lambda b, h, i: (b, 0, DV_BLK + h)),
            pl.BlockSpec((2, 5, LANES, LANES), lambda b, h, i: (h, 0, 0, 0)),
            pl.BlockSpec((1, LANES), lambda b, h, i: (0, 0)),
        ],
        out_specs=pl.BlockSpec((1, DIFF_TQ, LANES), lambda b, h, i: (b, i, h)),
        out_shape=jax.ShapeDtypeStruct((BATCH, SEQ, N_DIFF_HEADS * HEAD_DIM), BF16),
        compiler_params=_params(("parallel", "parallel", "arbitrary")),
        name="diff_attention",
    )(lam, proj, proj, proj, t5_tab, norm_g2)


NA_AHEAD = 4


def _na_kernel(q_ref, k_ref, v_ref, tab_ref, o_ref):
    lane = lax.broadcasted_iota(jnp.int32, (1, LANES), 1)
    h0 = lane < HEAD_DIM

    def win(r):
        return min(max(r - NA_WIN_H // 2, 0), GRID_ROWS - NA_WIN_H)

    def scores(r):
        rs = win(r)
        qr = q_ref[0, r * GRID_W:(r + 1) * GRID_W, :].astype(F32) * (HEAD_DIM ** -0.5)
        kw = k_ref[0, rs * GRID_W:rs * GRID_W + NA_KEYS, :]
        q2 = jnp.concatenate([jnp.where(h0, qr, 0.0), jnp.where(h0, 0.0, qr)], axis=0).astype(BF16)
        return lax.dot_general(q2, kw, NT_DIMS, preferred_element_type=F32) + tab_ref[0, r - rs]

    pending = [scores(r) for r in range(NA_AHEAD)]
    for r in range(GRID_ROWS):
        s = pending.pop(0)
        if r + NA_AHEAD < GRID_ROWS:
            pending.append(scores(r + NA_AHEAD))
        rs = win(r)
        vw = v_ref[0, rs * GRID_W:rs * GRID_W + NA_KEYS, :]
        e = jnp.exp(s - jnp.max(s, axis=-1, keepdims=True))
        p = e * (1.0 / jnp.sum(e, axis=-1, keepdims=True))
        o = jnp.dot(p.astype(BF16), vw, preferred_element_type=F32)
        o_ref[0, r * GRID_W:(r + 1) * GRID_W, :] = jnp.where(h0, o[:GRID_W], o[GRID_W:]).astype(BF16)


def na_attention(proj, na_tab):
    return pl.pallas_call(
        _na_kernel,
        grid=(BATCH, N_NA_HEADS // 2),
        in_specs=[
            pl.BlockSpec((1, SEQ, LANES), lambda b, h: (b, 0, NQ_BLK + h)),
            pl.BlockSpec((1, SEQ, LANES), lambda b, h: (b, 0, NK_BLK + h)),
            pl.BlockSpec((1, SEQ, LANES), lambda b, h: (b, 0, NV_BLK + h)),
            pl.BlockSpec((1, NA_WIN_H, 2 * GRID_W, NA_KEYS), lambda b, h: (h, 0, 0, 0)),
        ],
        out_specs=pl.BlockSpec((1, SEQ, LANES), lambda b, h: (b, 0, h)),
        out_shape=jax.ShapeDtypeStruct((BATCH, SEQ, N_NA_HEADS * HEAD_DIM), BF16),
        compiler_params=_params(("parallel", "parallel")),
        name="na_attention",
    )(proj, proj, proj, na_tab)


LN_TM = 512


def _layer_norm(y, g, b):
    mu = jnp.mean(y, axis=-1, keepdims=True)
    yc = y - mu
    var = jnp.mean(yc * yc, axis=-1, keepdims=True)
    return yc * lax.rsqrt(var + LN_EPS) * g + b


def _out_ln_kernel(od_ref, on_ref, x_ref, w_ref, g_ref, b_ref, y_ref, yt_ref):
    half = N_DIFF_HEADS * HEAD_DIM
    h = jnp.dot(od_ref[...], w_ref[0:half, :], preferred_element_type=F32)
    h = h + jnp.dot(on_ref[...], w_ref[half:2 * half, :], preferred_element_type=F32)
    y = _layer_norm(ALPHA * x_ref[...] + h, g_ref[...], b_ref[...])
    y_ref[...] = y
    yt_ref[0] = y.T.astype(BF16)


def out_proj_ln(o_diff, o_na, x2d, w, g, b):
    half = N_DIFF_HEADS * HEAD_DIM
    return pl.pallas_call(
        _out_ln_kernel,
        grid=(TOKENS // LN_TM,),
        in_specs=[
            pl.BlockSpec((LN_TM, half), lambda i: (i, 0)),
            pl.BlockSpec((LN_TM, half), lambda i: (i, 0)),
            pl.BlockSpec((LN_TM, D_MODEL), lambda i: (i, 0)),
            pl.BlockSpec((D_MODEL, D_MODEL), lambda i: (0, 0)),
            pl.BlockSpec((1, D_MODEL), lambda i: (0, 0)),
            pl.BlockSpec((1, D_MODEL), lambda i: (0, 0)),
        ],
        out_specs=[pl.BlockSpec((LN_TM, D_MODEL), lambda i: (i, 0)),
                   pl.BlockSpec((1, D_MODEL, LN_TM), lambda i: (i, 0, 0))],
        out_shape=[jax.ShapeDtypeStruct((TOKENS, D_MODEL), F32),
                   jax.ShapeDtypeStruct((TOKENS // LN_TM, D_MODEL, LN_TM), BF16)],
        compiler_params=_params(("parallel",)),
        name="out_proj_ln",
    )(o_diff, o_na, x2d, w, g, b)


RT_TM = LN_TM
EH_ROWS = PEER_N_KEYS * PEER_HEADS


def _cmp_exchange(xs, i, l):
    a, b = xs[i], xs[l]
    xs[i], xs[l] = jnp.maximum(a, b), jnp.minimum(a, b)


def _bitonic_merge_desc(xs):
    xs = list(xs)
    j = len(xs) // 2
    while j >= 1:
        for i in range(len(xs)):
            if i & j == 0:
                _cmp_exchange(xs, i, i | j)
        j //= 2
    return xs


def _sort16_desc(xs):
    xs = list(xs)
    k = 2
    while k <= PEER_TOPK:
        j = k // 2
        while j >= 1:
            for i in range(PEER_TOPK):
                l = i ^ j
                if l > i:
                    if i & k == 0:
                        _cmp_exchange(xs, i, l)
                    else:
                        _cmp_exchange(xs, l, i)
            j //= 2
        k *= 2
    return xs


def _top16_desc(xs):
    runs = [_sort16_desc(xs[g:g + PEER_TOPK]) for g in range(0, len(xs), PEER_TOPK)]
    while len(runs) > 1:
        nxt = []
        for a, b in zip(runs[0::2], runs[1::2]):
            nxt.append(_bitonic_merge_desc([jnp.maximum(a[i], b[PEER_TOPK - 1 - i])
                                            for i in range(PEER_TOPK)]))
        runs = nxt
    return runs[0]


BIG = 3.0e38


def _route_kernel(xt_ref, wqt_ref, keh_ref, c1_ref, p1_ref, s2_ref, p2_ref, s_scr, t_scr):
    qt = jnp.dot(wqt_ref[...], xt_ref[0], preferred_element_type=F32).astype(BF16)
    for a in range(2):
        s = jnp.dot(keh_ref[a], qt[a * EH_ROWS:(a + 1) * EH_ROWS], preferred_element_type=F32)
        for b in range(RT_TM // LANES):
            s_scr[a, b] = s[:, b * LANES:(b + 1) * LANES]

    def block(b, carry):
        s1 = s_scr[0, b].reshape(PEER_N_KEYS, PEER_HEADS, LANES)
        s2 = s_scr[1, b].reshape(PEER_N_KEYS, PEER_HEADS, LANES)
        v1 = _top16_desc([s1[e] for e in range(PEER_N_KEYS)])
        v2 = _top16_desc([s2[e] for e in range(PEER_N_KEYS)])
        cands = [v1[i] + v2[j] for (i, j) in STAIR]
        pad = [jnp.full_like(cands[0], -jnp.inf)] * (-len(cands) % PEER_TOPK)
        best = _top16_desc(cands + pad)
        top, thr = best[0], best[PEER_TOPK - 1]
        z = jnp.zeros_like(top)
        cut = [jnp.full_like(top, BIG) for _ in range(PEER_TOPK)]
        for (i, j), c in zip(STAIR, cands):
            sel = c >= thr
            z = z + jnp.where(sel, jnp.exp(c - top), 0.0)
            cut[i] = jnp.where(sel, jnp.minimum(cut[i], v2[j]), cut[i])
        inv_z = 1.0 / z
        c1 = jnp.full_like(s1, BIG)
        for i in range(PEER_TOPK):
            c1 = jnp.where(s1 == v1[i][None], cut[i][None], c1)
        c1_ref[b] = c1
        p1_ref[b] = jnp.exp(s1 - v1[0][None]) * inv_z[None]
        t_scr[...] = jnp.exp(s2 - v2[0][None]).reshape(EH_ROWS, LANES)
        for h in range(PEER_HEADS):
            s2_ref[b, h] = s_scr[1, b, pl.ds(h, PEER_N_KEYS, stride=PEER_HEADS), :]
            p2_ref[b, h] = t_scr[pl.ds(h, PEER_N_KEYS, stride=PEER_HEADS), :]
        return carry

    lax.fori_loop(0, RT_TM // LANES, block, 0)


def peer_route(xt, wqt, keh):
    nblk = TOKENS // LANES
    nb = RT_TM // LANES
    shp_kh = jax.ShapeDtypeStruct((nblk, PEER_N_KEYS, PEER_HEADS, LANES), F32)
    shp_hk = jax.ShapeDtypeStruct((nblk, PEER_HEADS, PEER_N_KEYS, LANES), F32)
    ospec_kh = pl.BlockSpec((nb, PEER_N_KEYS, PEER_HEADS, LANES), lambda i: (i, 0, 0, 0))
    ospec_hk = pl.BlockSpec((nb, PEER_HEADS, PEER_N_KEYS, LANES), lambda i: (i, 0, 0, 0))
    return pl.pallas_call(
        _route_kernel,
        grid=(TOKENS // RT_TM,),
        in_specs=[
            pl.BlockSpec((1, D_MODEL, RT_TM), lambda i: (i, 0, 0)),
            pl.BlockSpec((2 * EH_ROWS, D_MODEL), lambda i: (0, 0)),
            pl.BlockSpec((2, EH_ROWS, EH_ROWS), lambda i: (0, 0, 0)),
        ],
        out_specs=[ospec_kh, ospec_kh, ospec_hk, ospec_hk],
        out_shape=[shp_kh, shp_kh, shp_hk, shp_hk],
        scratch_shapes=[pltpu.VMEM((2, nb, EH_ROWS, LANES), F32), pltpu.VMEM((EH_ROWS, LANES), F32)],
        compiler_params=_params(("parallel",)),
        name="peer_route",
    )(xt, wqt, keh)


PE_SLAB = LN_TM
PE_NSLAB = 2
PE_TM = PE_NSLAB * PE_SLAB
PE_E1 = 8
PE_TE = PE_E1 * PEER_N_KEYS


def _gelu(x):
    return x * (lax.erf(x * (1.0 / math.sqrt(2.0))) + 1.0) * 0.5


PE_TILES = PEER_N_KEYS // PE_E1
PE_E1G = 2
PE_CHUNK = 4 * PEER_N_KEYS
PE_NCH = PE_TE // PE_CHUNK
PE_DROWS = D_MODEL // PE_NCH
PE_K2 = 64
PE_LAG3 = 2


def _peer_chunk(idx, xt_ref, ut_ref, v_ref, c1_ref, p1_ref, s2_ref, p2_ref, o_ref, h_new, h_old, w_new, w_old):
    sl = idx // PE_NCH
    c = idx % PE_NCH
    c0 = pl.multiple_of(c * PE_CHUNK, PE_CHUNK)
    d0 = pl.multiple_of(c * PE_DROWS, PE_DROWS)
    for tc in range(PE_SLAB // LANES):
        tb = sl * (PE_SLAB // LANES) + tc
        ts = pl.ds(tc * LANES, LANES)
        for e0 in range(0, PE_CHUNK // PEER_N_KEYS, PE_E1G):
            for k0 in range(0, PEER_N_KEYS, PE_K2):
                g = [None] * PE_E1G
                for h in range(PEER_HEADS):
                    s2 = s2_ref[tb, h, pl.ds(k0, PE_K2), :]
                    p2 = p2_ref[tb, h, pl.ds(k0, PE_K2), :]
                    for e in range(PE_E1G):
                        el = c * (PE_CHUNK // PEER_N_KEYS) + e0 + e
                        c1 = c1_ref[tb, el, pl.ds(h, 1), :]
                        p1 = p1_ref[tb, el, pl.ds(h, 1), :]
                        t = jnp.where(s2 >= c1, p1 * p2, 0.0)
                        g[e] = t if g[e] is None else g[e] + t
                for e in range(PE_E1G):
                    rows = pl.ds(c0 + (e0 + e) * PEER_N_KEYS + k0, PE_K2)
                    w_new[sl, rows, ts] = (g[e] * _gelu(h_old[sl, rows, ts])).astype(BF16)
    h_new[sl, pl.ds(c0, PE_CHUNK), :] = lax.dot_general(ut_ref[c], xt_ref[sl], TN_DIMS,
                                                        preferred_element_type=F32)
    o_ref[sl, pl.ds(d0, PE_DROWS), :] += lax.dot_general(v_ref[c], w_old[sl], TN_DIMS,
                                                         preferred_element_type=F32)


def _peer_step(*refs):
    def body(idx, carry):
        _peer_chunk(idx, *refs)
        return carry

    lax.fori_loop(0, PE_NSLAB * PE_NCH, body, 0)


def _peer_kernel(xt_ref, ut_ref, v_ref, c1_ref, p1_ref, s2_ref, p2_ref, o_ref, h_a, h_b, w_a, w_b):
    s = pl.program_id(0)

    @pl.when(s == 0)
    def _():
        h_b[...] = jnp.zeros_like(h_b)
        w_b[...] = jnp.zeros_like(w_b)

    @pl.when((s == 0) | (s % PE_TILES == PE_LAG3))
    def _():
        o_ref[...] = jnp.zeros_like(o_ref)

    args = (xt_ref, ut_ref, v_ref, c1_ref, p1_ref, s2_ref, p2_ref, o_ref)

    @pl.when(s % 2 == 0)
    def _():
        _peer_step(*args, h_a, h_b, w_a, w_b)

    @pl.when(s % 2 == 1)
    def _():
        _peer_step(*args, h_b, h_a, w_b, w_a)


def peer_experts(xt, ut, v, c1, p1, s2, p2):
    n_items = (TOKENS // PE_TM) * PE_TILES
    nb = PE_TM // LANES

    def item(s, lag):
        w = jnp.clip(s - lag, 0, n_items - 1)
        return w // PE_TILES, w % PE_TILES

    gate_e1 = pl.BlockSpec((nb, PE_E1, PEER_HEADS, LANES), lambda s: (*item(s, 1), 0, 0))
    gate_e2 = pl.BlockSpec((nb, PEER_HEADS, PEER_N_KEYS, LANES), lambda s: (item(s, 1)[0], 0, 0, 0))
    return pl.pallas_call(
        _peer_kernel,
        grid=(n_items + PE_LAG3,),
        in_specs=[
            pl.BlockSpec((PE_NSLAB, D_MODEL, PE_SLAB), lambda s: (item(s, 0)[0], 0, 0)),
            pl.BlockSpec((PE_NCH, D_MODEL, PE_CHUNK), lambda s: (item(s, 0)[1], 0, 0)),
            pl.BlockSpec((PE_NCH, PE_TE, PE_DROWS), lambda s: (item(s, PE_LAG3)[1], 0, 0)),
            gate_e1, gate_e1, gate_e2, gate_e2,
        ],
        out_specs=pl.BlockSpec((PE_NSLAB, D_MODEL, PE_SLAB), lambda s: (item(s, PE_LAG3)[0], 0, 0)),
        out_shape=jax.ShapeDtypeStruct((TOKENS // PE_SLAB, D_MODEL, PE_SLAB), F32),
        scratch_shapes=[pltpu.VMEM((PE_NSLAB, PE_TE, PE_SLAB), F32), pltpu.VMEM((PE_NSLAB, PE_TE, PE_SLAB), F32),
                        pltpu.VMEM((PE_NSLAB, PE_TE, PE_SLAB), BF16), pltpu.VMEM((PE_NSLAB, PE_TE, PE_SLAB), BF16)],
        compiler_params=_params(("arbitrary",)),
        name="peer_experts",
    )(xt, ut, v, c1, p1, s2, p2)


def _res_ln_kernel(x_ref, ft_ref, g_ref, b_ref, y_ref):
    y_ref[...] = _layer_norm(ALPHA * x_ref[...] + ft_ref[0].T, g_ref[...], b_ref[...])


def res_ln(x2d, ft, g, b):
    return pl.pallas_call(
        _res_ln_kernel,
        grid=(TOKENS // LN_TM,),
        in_specs=[
            pl.BlockSpec((LN_TM, D_MODEL), lambda i: (i, 0)),
            pl.BlockSpec((1, D_MODEL, LN_TM), lambda i: (i, 0, 0)),
            pl.BlockSpec((1, D_MODEL), lambda i: (0, 0)),
            pl.BlockSpec((1, D_MODEL), lambda i: (0, 0)),
        ],
        out_specs=pl.BlockSpec((LN_TM, D_MODEL), lambda i: (i, 0)),
        out_shape=jax.ShapeDtypeStruct((TOKENS, D_MODEL), F32),
        compiler_params=_params(("parallel",)),
        name="res_ln",
    )(x2d, ft, g, b)


def _cast_kernel(x_ref, o_ref):
    o_ref[0] = x_ref[...].astype(BF16)


def _cast_t_kernel(x_ref, o_ref):
    o_ref[0] = x_ref[...].T.astype(BF16)


def expert_up_layout(u, l):
    n = PEER_N_EXPERTS // PE_CHUNK
    return pl.pallas_call(
        _cast_t_kernel,
        grid=(n,),
        in_specs=[pl.BlockSpec((None, PE_CHUNK, D_MODEL), lambda i: (l, i, 0))],
        out_specs=pl.BlockSpec((1, D_MODEL, PE_CHUNK), lambda i: (i, 0, 0)),
        out_shape=jax.ShapeDtypeStruct((n, D_MODEL, PE_CHUNK), BF16),
        compiler_params=_params(("parallel",)),
        name="expert_up_layout",
    )(u)


def expert_down_layout(v, l):
    return pl.pallas_call(
        _cast_kernel,
        grid=(PE_TILES, PE_NCH),
        in_specs=[pl.BlockSpec((None, PE_TE, PE_DROWS), lambda i, j: (l, i, j))],
        out_specs=pl.BlockSpec((1, PE_TE, PE_DROWS), lambda i, j: (i * PE_NCH + j, 0, 0)),
        out_shape=jax.ShapeDtypeStruct((PE_TILES * PE_NCH, PE_TE, PE_DROWS), BF16),
        compiler_params=_params(("parallel", "parallel")),
        name="expert_down_layout",
    )(v)


def query_proj_layout(wq, l):
    return pl.pallas_call(
        _cast_t_kernel,
        grid=(2, PEER_HEADS),
        in_specs=[pl.BlockSpec((None, D_MODEL, PEER_N_KEYS), lambda a, h: (l, 0, 2 * h + a))],
        out_specs=pl.BlockSpec((1, PEER_N_KEYS, D_MODEL), lambda a, h: (a * PEER_HEADS + h, 0, 0)),
        out_shape=jax.ShapeDtypeStruct((2 * PEER_HEADS, PEER_N_KEYS, D_MODEL), BF16),
        compiler_params=_params(("parallel", "parallel")),
        name="query_proj_layout",
    )(wq).reshape(2 * EH_ROWS, D_MODEL)


def _t5_bucket(rel):
    half = T5_BUCKETS // 2
    base = jnp.where(rel > 0, half, 0)
    n = jnp.abs(rel)
    nf = jnp.maximum(n, 1).astype(F32)
    large = T5_MAX_EXACT + (jnp.log(nf / T5_MAX_EXACT) / math.log(T5_MAX_DIST / T5_MAX_EXACT)
                            * (half - T5_MAX_EXACT)).astype(jnp.int32)
    large = jnp.minimum(large, half - 1)
    return base + jnp.where(n < T5_MAX_EXACT, n, large)


def _t5_tiles(t5_bias):
    r = jnp.arange(LANES, dtype=jnp.int32)
    blk = jnp.arange(-2, 3, dtype=jnp.int32)
    rel = blk[:, None, None] * LANES + r[None, None, :] - r[None, :, None]
    onehot = (_t5_bucket(rel)[..., None] == jnp.arange(T5_BUCKETS, dtype=jnp.int32)).astype(F32)
    return jnp.einsum('tqkc,ch->htqk', onehot, t5_bias.astype(F32), precision=lax.Precision.HIGHEST)


def _na_tiles(rpb):
    c = jnp.arange(GRID_W, dtype=jnp.int32)
    col_start = jnp.clip(c - NA_WIN_W // 2, 0, GRID_W - NA_WIN_W)
    in_win = (c[None, :] >= col_start[:, None]) & (c[None, :] < col_start[:, None] + NA_WIN_W)
    dc = jnp.clip(c[None, :] - c[:, None] + (NA_WIN_W - 1), 0, 2 * NA_WIN_W - 2)
    onehot = (dc[..., None] == jnp.arange(2 * NA_WIN_W - 1, dtype=jnp.int32)).astype(F32)
    rows = jnp.stack([rpb[:, NA_WIN_H - 1 - off:2 * NA_WIN_H - 1 - off, :] for off in range(NA_WIN_H)],
                     axis=1).astype(F32)
    bias = jnp.einsum('hoic,qkc->hoqik', rows, onehot, precision=lax.Precision.HIGHEST)
    bias = jnp.where(in_win[None, None, :, None, :], bias, NEG_INF)
    bias = bias.reshape(N_NA_HEADS // 2, 2, NA_WIN_H, GRID_W, NA_KEYS)
    return bias.transpose(0, 2, 1, 3, 4).reshape(N_NA_HEADS // 2, NA_WIN_H, 2 * GRID_W, NA_KEYS)


def kernel(x, w_in, w_out, lam_q1, lam_k1, lam_q2, lam_k2, diff_norm_g, t5_bias, na_rpb,
           ln1_g, ln1_b, peer_wq, peer_keys, peer_u, peer_v, ln2_g, ln2_b):
    xf = x.reshape(TOKENS, D_MODEL)
    t5_tab = _t5_tiles(t5_bias) * LOG2E
    eye = jnp.eye(PEER_HEADS, dtype=F32)
    for l in range(DEPTH):
        lam_init = 0.8 - 0.6 * math.exp(-0.3 * l)
        lam = (jnp.exp(jnp.sum(lam_q1[l] * lam_k1[l])) - jnp.exp(jnp.sum(lam_q2[l] * lam_k2[l]))
               + lam_init).reshape(1).astype(F32)
        proj = in_proj(xf, w_in[l].astype(BF16)).reshape(BATCH, SEQ, IN_PROJ_W)
        g2 = jnp.tile(diff_norm_g[l], 2).reshape(1, LANES)
        o_diff = diff_attention(proj, lam, t5_tab, g2, lam_init)
        o_na = na_attention(proj, _na_tiles(na_rpb[l]))
        x1, x1t = out_proj_ln(o_diff.reshape(TOKENS, -1), o_na.reshape(TOKENS, -1), xf,
                              w_out[l].astype(BF16), ln1_g[l].reshape(1, -1), ln1_b[l].reshape(1, -1))
        keh = jnp.einsum('aed,hg->aehgd', peer_keys[l], eye).reshape(2, EH_ROWS, EH_ROWS).astype(BF16)
        c1, p1, s2, p2 = peer_route(x1t, query_proj_layout(peer_wq, l), keh)
        ft = peer_experts(x1t, expert_up_layout(peer_u, l), expert_down_layout(peer_v, l), c1, p1, s2, p2)
        xf = res_ln(x1, ft, ln2_g[l].reshape(1, -1), ln2_b[l].reshape(1, -1))
    return xf.reshape(BATCH, SEQ, D_MODEL)
```
